```python
import math
import jax, jax.numpy as jnp
from jax import lax
import numpy as np

D_MODEL = 1024
BATCH = 8
SEQ = 4096
DEPTH = 1

N_META = 16
GDN_HEADS = D_MODEL // 256
GDN_DK = 128
GDN_DV = 128
FOX_HEADS = D_MODEL // 128
FOX_DH = 64
D_MIX = GDN_HEADS * GDN_DV + FOX_HEADS * FOX_DH
CONV_K = 4
CHUNK = 64
BLOCK_Q = 128
D_FF = -(-8 * D_MODEL // (3 * 256)) * 256
EPS = 1e-6
MASK_VALUE = -1e30
GDN_QKV = 2 * GDN_HEADS * GDN_DK + GDN_HEADS * GDN_DV
IN_SIZES = (GDN_HEADS * GDN_DK, GDN_HEADS * GDN_DK, GDN_HEADS * GDN_DV, GDN_HEADS * GDN_DV,
            GDN_HEADS, GDN_HEADS,
            FOX_HEADS * FOX_DH, FOX_HEADS * FOX_DH, FOX_HEADS * FOX_DH, FOX_HEADS)
D_IN = sum(IN_SIZES)
IN_SPLITS = tuple(int(s) for s in np.cumsum(IN_SIZES)[:-1])

kernel_name = "hymba_gdn_fox_hybrid_layer"


def rmsnorm(x, w):
    x32 = x.astype(jnp.float32)
    y = x32 * lax.rsqrt(jnp.mean(x32 * x32, axis=-1, keepdims=True) + EPS)
    return (y * w.astype(jnp.float32)).astype(x.dtype)


def l2norm(x):
    return x * lax.rsqrt(jnp.sum(x * x, axis=-1, keepdims=True) + EPS)


def causal_dwconv(x, w):
    k = w.shape[0]
    return lax.conv_general_dilated(x, w[:, None, :], window_strides=(1,), padding=[(k - 1, 0)],
                                    dimension_numbers=("NWC", "WIO", "NWC"),
                                    feature_group_count=x.shape[-1])


def chunk_gated_delta_rule(q, k, v, beta, g, s0, chunk):
    b, h, t, _ = q.shape
    dv = v.shape[-1]
    n = t // chunk
    q, k, v = (a.reshape(b, h, n, chunk, a.shape[-1]) for a in (q, k, v))
    beta = beta.reshape(b, h, n, chunk)
    gc = jnp.cumsum(g.reshape(b, h, n, chunk), axis=-1)
    incl = jnp.tril(jnp.ones((chunk, chunk), dtype=bool))
    strict = jnp.tril(jnp.ones((chunk, chunk), dtype=bool), -1)
    diff = gc[..., :, None] - gc[..., None, :]
    decay = jnp.where(incl, jnp.exp(jnp.where(incl, diff, 0.0)), 0.0)
    a_mat = jnp.where(strict, jnp.einsum("bhncd,bhnsd->bhncs", k, k) * decay * beta[..., :, None], 0.0)
    rhs = jnp.concatenate([v * beta[..., None], k * (beta * jnp.exp(gc))[..., None]], axis=-1)
    sol = lax.linalg.triangular_solve(a_mat + jnp.eye(chunk, dtype=a_mat.dtype), rhs,
                                      left_side=True, lower=True, unit_diagonal=True)
    u, w = sol[..., :dv], sol[..., dv:]
    qk = jnp.einsum("bhncd,bhnsd->bhncs", q, k) * decay
    xs = tuple(jnp.moveaxis(a, 2, 0) for a in (q, k, u, w, gc, qk))

    def step(state, inp):
        q_c, k_c, u_c, w_c, g_c, qk_c = inp
        v_new = u_c - jnp.einsum("bhcd,bhde->bhce", w_c, state)
        o_c = (jnp.einsum("bhcd,bhde->bhce", q_c * jnp.exp(g_c)[..., None], state)
               + jnp.einsum("bhcs,bhse->bhce", qk_c, v_new))
        g_last = g_c[..., -1:]
        state = (state * jnp.exp(g_last)[..., None]
                 + jnp.einsum("bhcd,bhce->bhde", k_c * jnp.exp(g_last - g_c)[..., None], v_new))
        return state, o_c

    s_final, o = lax.scan(step, s0, xs)
    o = jnp.moveaxis(o, 0, 2).reshape(b, h, t, dv)
    return o, s_final


def gated_deltanet_group(qkv, z, b_raw, a_raw, conv_w, a_log, dt_bias, norm_w):
    bsz, t, _ = qkv.shape
    f32 = jnp.float32
    qkv = jax.nn.silu(causal_dwconv(qkv, conv_w)).astype(f32)
    q, k, v = jnp.split(qkv, [GDN_HEADS * GDN_DK, 2 * GDN_HEADS * GDN_DK], axis=-1)
    q = l2norm(q.reshape(bsz, t, GDN_HEADS, GDN_DK)).transpose(0, 2, 1, 3) * (GDN_DK ** -0.5)
    k = l2norm(k.reshape(bsz, t, GDN_HEADS, GDN_DK)).transpose(0, 2, 1, 3)
    v = v.reshape(bsz, t, GDN_HEADS, GDN_DV).transpose(0, 2, 1, 3)
    beta = jax.nn.sigmoid(b_raw.astype(f32)).transpose(0, 2, 1)
    g = (-jnp.exp(a_log.astype(f32))
         * jax.nn.softplus(a_raw.astype(f32) + dt_bias.astype(f32))).transpose(0, 2, 1)
    s0 = jnp.zeros((bsz, GDN_HEADS, GDN_DK, GDN_DV), f32)
    o_meta, s_meta = chunk_gated_delta_rule(q[:, :, :N_META], k[:, :, :N_META], v[:, :, :N_META],
                                            beta[:, :, :N_META], g[:, :, :N_META], s0, N_META)
    o_real, _ = chunk_gated_delta_rule(q[:, :, N_META:], k[:, :, N_META:], v[:, :, N_META:],
                                       beta[:, :, N_META:], g[:, :, N_META:], s_meta, CHUNK)
    o = jnp.concatenate([o_meta, o_real], axis=2).transpose(0, 2, 1, 3)
    zg = jax.nn.silu(z.astype(f32).reshape(bsz, t, GDN_HEADS, GDN_DV))
    o = o * lax.rsqrt(jnp.mean(o * o, axis=-1, keepdims=True) + EPS) * norm_w.astype(f32) * zg
    return o.reshape(bsz, t, GDN_HEADS * GDN_DV).astype(qkv.dtype)


def forgetting_attention_group(q, k, v, f_raw, f_bias):
    bsz, t, _ = q.shape
    f32 = jnp.float32
    q, k, v = (a.astype(f32).reshape(bsz, t, FOX_HEADS, FOX_DH) for a in (q, k, v))
    logf = jax.nn.log_sigmoid(f_raw.astype(f32) + f_bias.astype(f32))
    c = jnp.cumsum(logf, axis=1).transpose(0, 2, 1)
    scale = FOX_DH ** -0.5
    n_blocks = (t - N_META) // BLOCK_Q
    bounds = [0, N_META] + [N_META + (i + 1) * BLOCK_Q for i in range(n_blocks)]
    outs = []
    for s, e in zip(bounds[:-1], bounds[1:]):
        sc = (jnp.einsum("bqhd,bkhd->bhqk", q[:, s:e], k[:, :e]) * scale
              + c[:, :, s:e, None] - c[:, :, None, :e])
        causal = jnp.arange(s, e)[:, None] >= jnp.arange(e)[None, :]
        p = jax.nn.softmax(jnp.where(causal, sc, MASK_VALUE), axis=-1)
        outs.append(jnp.einsum("bhqk,bkhd->bqhd", p, v[:, :e]))
    return jnp.concatenate(outs, axis=1).reshape(bsz, t, FOX_HEADS * FOX_DH)


def swiglu(x, w_gate, w_up, w_down):
    return (jax.nn.silu(x @ w_gate) * (x @ w_up)) @ w_down


def setup_inputs(seed: int = 0) -> dict:
    key = jax.random.key(seed)
    ks = jax.random.split(key, 15)
    nrm = jax.random.normal
    dt = jnp.exp(jax.random.uniform(ks[6], (DEPTH, GDN_HEADS), minval=math.log(1e-3), maxval=math.log(1e-1)))
    return {
        "x": nrm(ks[0], (BATCH, SEQ, D_MODEL), jnp.float32),
        "meta_tokens": nrm(ks[1], (N_META, D_MODEL), jnp.float32),
        "attn_norm_w": 1.0 + 0.02 * nrm(ks[2], (DEPTH, D_MODEL), jnp.float32),
        "w_in": nrm(ks[3], (DEPTH, D_MODEL, D_IN), jnp.float32) * D_MODEL ** -0.5,
        "conv_w": nrm(ks[4], (DEPTH, CONV_K, GDN_QKV), jnp.float32) * CONV_K ** -0.5,
        "a_log": jnp.log(jax.random.uniform(ks[5], (DEPTH, GDN_HEADS), minval=1.0, maxval=16.0)),
        "dt_bias": dt + jnp.log(-jnp.expm1(-dt)),
        "gdn_norm_w": 1.0 + 0.02 * nrm(ks[7], (DEPTH, GDN_DV), jnp.float32),
        "fgate_b": 2.0 + 0.5 * nrm(ks[8], (DEPTH, FOX_HEADS), jnp.float32),
        "w_out": nrm(ks[9], (DEPTH, D_MIX, D_MODEL), jnp.float32) * D_MIX ** -0.5,
        "ffn_norm_w": 1.0 + 0.02 * nrm(ks[10], (DEPTH, D_MODEL), jnp.float32),
        "w_gate": nrm(ks[11], (DEPTH, D_MODEL, D_FF), jnp.float32) * D_MODEL ** -0.5,
        "w_up": nrm(ks[12], (DEPTH, D_MODEL, D_FF), jnp.float32) * D_MODEL ** -0.5,
        "w_down": nrm(ks[13], (DEPTH, D_FF, D_MODEL), jnp.float32) * D_FF ** -0.5,
        "final_norm_w": 1.0 + 0.02 * nrm(ks[14], (D_MODEL,), jnp.float32),
    }


def reference(x, meta_tokens, attn_norm_w, w_in, conv_w, a_log, dt_bias, gdn_norm_w, fgate_b,
              w_out, ffn_norm_w, w_gate, w_up, w_down, final_norm_w):
    bsz = x.shape[0]
    meta = jnp.broadcast_to(meta_tokens[None].astype(x.dtype), (bsz, N_META, D_MODEL))
    h = jnp.concatenate([meta, x], axis=1)
    for l in range(DEPTH):
        u = rmsnorm(h, attn_norm_w[l])
        proj = u @ w_in[l]
        gq, gk, gv, gz, gb, ga, fq, fk, fv, ff = jnp.split(proj, IN_SPLITS, axis=-1)
        o_gdn = gated_deltanet_group(jnp.concatenate([gq, gk, gv], axis=-1), gz, gb, ga,
                                     conv_w[l], a_log[l], dt_bias[l], gdn_norm_w[l])
        o_fox = forgetting_attention_group(fq, fk, fv, ff, fgate_b[l]).astype(h.dtype)
        h = h + jnp.concatenate([o_gdn, o_fox], axis=-1) @ w_out[l]
        h = h + swiglu(rmsnorm(h, ffn_norm_w[l]), w_gate[l], w_up[l], w_down[l])
    h = rmsnorm(h, final_norm_w)
    return h[:, N_META:]
```

```python
import functools
import math

import jax
import jax.numpy as jnp
from jax import lax
from jax.experimental import pallas as pl
from jax.experimental.pallas import tpu as pltpu

F32 = jnp.float32
BF16 = jnp.bfloat16

D_MODEL = 1024
N_META = 16
GDN_HEADS = 4
GDN_D = 128
FOX_HEADS = 8
FOX_DH = 64
CONV_K = 4
CHUNK = 64
EPS = 1e-6
MASK_VALUE = -1e30
GDN_QKV = 3 * GDN_HEADS * GDN_D
GDN_HD = GDN_HEADS * GDN_D
FOX_HD = FOX_HEADS * FOX_DH
LANES = 128
SUBLANES = 8
LANE_BETA = 0
LANE_DECAY = GDN_HEADS
LANE_FORGET = 2 * GDN_HEADS
VMEM_LIMIT = 56 * 1024 * 1024


def _nt_dot(a, b):
    return lax.dot_general(a, b, (((1,), (1,)), ((), ())), preferred_element_type=F32)


def _tn_dot(a, b):
    return lax.dot_general(a, b, (((0,), (0,)), ((), ())), preferred_element_type=F32)


def _dot(a, b):
    return jnp.dot(a, b, preferred_element_type=F32)


def _dot_hi(a, b):
    return jnp.dot(a, b, preferred_element_type=F32, precision=lax.Precision.HIGHEST)


def _split3(x):
    x1 = x.astype(BF16)
    r1 = x - x1.astype(F32)
    x2 = r1.astype(BF16)
    x3 = (r1 - x2.astype(F32)).astype(BF16)
    return x1, x2, x3


def _softplus(x):
    return jnp.maximum(x, 0.0) + jnp.log1p(jnp.exp(-jnp.abs(x)))


def _sigmoid(x):
    return 1.0 / (1.0 + jnp.exp(-x))


def _silu(x):
    return x * _sigmoid(x)


def _inproj_body(x_ref, nw_ref, wbig_ref, wsm_ref, qkv_ref, z_ref, fq_ref, fk_ref, fv_ref, sm_ref):
    x = x_ref[...]
    u = (x * lax.rsqrt(jnp.mean(x * x, axis=-1, keepdims=True) + EPS) * nw_ref[...]).astype(BF16)
    o0 = GDN_QKV
    o1 = o0 + GDN_HD
    o2 = o1 + FOX_HD
    o3 = o2 + FOX_HD
    o4 = o3 + FOX_HD
    qkv_ref[...] = _dot(u, wbig_ref[:, 0:o0])
    z_ref[...] = _dot(u, wbig_ref[:, o0:o1])
    fq_ref[...] = (_dot(u, wbig_ref[:, o1:o2]) * (FOX_DH ** -0.5)).astype(BF16)
    fk_ref[...] = _dot(u, wbig_ref[:, o2:o3]).astype(BF16)
    fv_ref[...] = _dot(u, wbig_ref[:, o3:o4]).astype(BF16)
    sm_ref[...] = _dot(u, wsm_ref[...])


def _inproj(x2d, nw, wbig, wsm, tm):
    rows = x2d.shape[0]
    nbig = wbig.shape[1]
    row_spec = lambda n: pl.BlockSpec((tm, n), lambda i: (i, 0))
    const_spec = lambda a: pl.BlockSpec(a.shape, lambda i: (0, 0))
    return pl.pallas_call(
        _inproj_body,
        grid=(rows // tm,),
        in_specs=[row_spec(D_MODEL), const_spec(nw), const_spec(wbig), const_spec(wsm)],
        out_specs=[row_spec(GDN_QKV), row_spec(GDN_HD), row_spec(FOX_HD), row_spec(FOX_HD), row_spec(FOX_HD),
                   row_spec(LANES)],
        out_shape=[jax.ShapeDtypeStruct((rows, GDN_QKV), F32), jax.ShapeDtypeStruct((rows, GDN_HD), F32),
                   jax.ShapeDtypeStruct((rows, FOX_HD), BF16), jax.ShapeDtypeStruct((rows, FOX_HD), BF16),
                   jax.ShapeDtypeStruct((rows, FOX_HD), BF16), jax.ShapeDtypeStruct((rows, LANES), F32)],
        compiler_params=pltpu.CompilerParams(dimension_semantics=("arbitrary",), vmem_limit_bytes=VMEM_LIMIT),
        name="inproj",
    )(x2d, nw, wbig, wsm)


def _gdn_body(qkv_ref, z_ref, sm_ref, convw_ref, hist_ref, s0_ref, avec_ref, dtvec_ref, normw_ref,
              o_ref, sout_ref, state_ref, xpad_ref, *, rows, chunk):
    j = pl.program_id(1)

    @pl.when(j == 0)
    def _():
        state_ref[...] = s0_ref[...]
        xpad_ref[0:SUBLANES, :] = hist_ref[...]

    xpad_ref[SUBLANES:SUBLANES + rows, :] = qkv_ref[...]
    conv = convw_ref[0:1, :] * xpad_ref[SUBLANES - 3:SUBLANES - 3 + rows, :]
    for kk in range(1, CONV_K):
        conv = conv + convw_ref[kk:kk + 1, :] * xpad_ref[SUBLANES - 3 + kk:SUBLANES - 3 + kk + rows, :]
    xpad_ref[0:SUBLANES, :] = xpad_ref[rows:rows + SUBLANES, :]
    act = _silu(conv)

    sm = sm_ref[...]
    beta_all = _sigmoid(sm)
    g_all = avec_ref[...] * _softplus(sm + dtvec_ref[...])

    r_i = lax.broadcasted_iota(jnp.int32, (rows, rows), 0)
    c_i = lax.broadcasted_iota(jnp.int32, (rows, rows), 1)
    tri = jnp.where((r_i >= c_i) & ((r_i // chunk) == (c_i // chunk)), 1.0, 0.0).astype(BF16)
    g1, g2, g3 = _split3(g_all)
    gc_all = _dot(tri, g1) + _dot(tri, g2) + _dot(tri, g3)
    sel = jnp.where(lax.broadcasted_iota(jnp.int32, (SUBLANES, LANES), 1)
                    == lax.broadcasted_iota(jnp.int32, (SUBLANES, LANES), 0) + LANE_DECAY, 1.0, 0.0).astype(BF16)
    c1, c2, c3 = _split3(gc_all)
    gc_rows = _nt_dot(sel, c1) + _nt_dot(sel, c2) + _nt_dot(sel, c3)
    egc_all = jnp.exp(gc_all)

    ci_r = lax.broadcasted_iota(jnp.int32, (chunk, chunk), 0)
    ci_c = lax.broadcasted_iota(jnp.int32, (chunk, chunk), 1)
    incl = ci_r >= ci_c
    strict = ci_r > ci_c
    eye = jnp.where(ci_r == ci_c, 1.0, 0.0).astype(F32)
    n_double = int(math.log2(chunk)) - 1

    normw = normw_ref[...]
    for c in range(rows // chunk):
        lo = c * chunk
        hi = lo + chunk
        gl_row = gc_all[hi - 1:hi, :]
        ekg_all = jnp.exp(gl_row - gc_all[lo:hi, :])
        egl_all = jnp.exp(gl_row)
        for h in range(GDN_HEADS):
            q = act[lo:hi, h * GDN_D:(h + 1) * GDN_D]
            k = act[lo:hi, GDN_HD + h * GDN_D:GDN_HD + (h + 1) * GDN_D]
            v = act[lo:hi, 2 * GDN_HD + h * GDN_D:2 * GDN_HD + (h + 1) * GDN_D]
            q = q * lax.rsqrt(jnp.sum(q * q, axis=-1, keepdims=True) + EPS) * (GDN_D ** -0.5)
            k = k * lax.rsqrt(jnp.sum(k * k, axis=-1, keepdims=True) + EPS)
            beta = beta_all[lo:hi, LANE_BETA + h:LANE_BETA + h + 1]
            gcol = gc_all[lo:hi, LANE_DECAY + h:LANE_DECAY + h + 1]
            egc = egc_all[lo:hi, LANE_DECAY + h:LANE_DECAY + h + 1]
            ekg = ekg_all[:, LANE_DECAY + h:LANE_DECAY + h + 1]
            egl = egl_all[:, LANE_DECAY + h:LANE_DECAY + h + 1]
            grow = gc_rows[h:h + 1, lo:hi]
            diff = gcol - grow
            decay = jnp.where(incl, jnp.exp(jnp.where(incl, diff, 0.0)), 0.0)
            kb = k * beta
            k16 = k.astype(BF16)
            a_mat = jnp.where(strict, _nt_dot(kb.astype(BF16), k16) * decay, 0.0)
            p_pow = -a_mat
            t_inv = eye + p_pow
            for _ in range(n_double):
                p_pow = _dot_hi(p_pow, p_pow)
                t_inv = t_inv + _dot_hi(t_inv, p_pow)
            u = _dot_hi(t_inv, v * beta)
            w = _dot_hi(t_inv, kb * egc)
            qk = jnp.where(incl, _nt_dot(q.astype(BF16), k16) * decay, 0.0)
            s_h = state_ref[h]
            s16 = s_h.astype(BF16)
            v_new = u - _dot(w.astype(BF16), s16)
            vn16 = v_new.astype(BF16)
            o = _dot((q * egc).astype(BF16), s16) + _dot(qk.astype(BF16), vn16)
            state_ref[h] = s_h * egl + _tn_dot((k * ekg).astype(BF16), vn16)
            zg = _silu(z_ref[lo:hi, h * GDN_D:(h + 1) * GDN_D])
            o = o * lax.rsqrt(jnp.mean(o * o, axis=-1, keepdims=True) + EPS) * normw * zg
            o_ref[lo:hi, h * GDN_D:(h + 1) * GDN_D] = o.astype(o_ref.dtype)

    @pl.when(j == pl.num_programs(1) - 1)
    def _():
        sout_ref[0] = state_ref[...]


def _gdn(qkv, z, sm, convw, hist, s0, avec, dtvec, normw, *, batch, seq, rows, chunk):
    nblk = seq // rows
    row_spec = lambda n: pl.BlockSpec((rows, n), lambda b, j: (b * nblk + j, 0))
    const2 = lambda a: pl.BlockSpec(a.shape, lambda b, j: (0, 0))
    body = functools.partial(_gdn_body, rows=rows, chunk=chunk)
    return pl.pallas_call(
        body,
        grid=(batch, nblk),
        in_specs=[row_spec(GDN_QKV), row_spec(GDN_HD), row_spec(LANES), const2(convw), const2(hist),
                  pl.BlockSpec(s0.shape, lambda b, j: (0, 0, 0)), const2(avec), const2(dtvec), const2(normw)],
        out_specs=[row_spec(GDN_HD), pl.BlockSpec((1, GDN_HEADS, GDN_D, GDN_D), lambda b, j: (b, 0, 0, 0))],
        out_shape=[jax.ShapeDtypeStruct((batch * seq, GDN_HD), BF16),
                   jax.ShapeDtypeStruct((batch, GDN_HEADS, GDN_D, GDN_D), F32)],
        scratch_shapes=[pltpu.VMEM((GDN_HEADS, GDN_D, GDN_D), F32),
                        pltpu.VMEM((rows + SUBLANES, GDN_QKV), F32)],
        compiler_params=pltpu.CompilerParams(dimension_semantics=("arbitrary", "arbitrary"),
                                             vmem_limit_bytes=VMEM_LIMIT),
        name="gdn",
    )(qkv, z, sm, convw, hist, s0, avec, dtvec, normw)


def _cumgate_body(sm_ref, fb_ref, c0_ref, ccol_ref, crow_ref, *, seq, blk):
    lane = lax.broadcasted_iota(jnp.int32, (1, LANES), 1)
    valid = (lane >= LANE_FORGET) & (lane < LANE_FORGET + FOX_HEADS)
    r_i = lax.broadcasted_iota(jnp.int32, (blk, blk), 0)
    c_i = lax.broadcasted_iota(jnp.int32, (blk, blk), 1)
    tri = jnp.where(r_i >= c_i, 1.0, 0.0).astype(BF16)
    sel = jnp.where(lax.broadcasted_iota(jnp.int32, (SUBLANES, LANES), 1)
                    == lax.broadcasted_iota(jnp.int32, (SUBLANES, LANES), 0) + LANE_FORGET, 1.0, 0.0).astype(BF16)
    carry = c0_ref[...]
    for i in range(seq // blk):
        xg = sm_ref[i * blk:(i + 1) * blk, :] + fb_ref[...]
        logf = jnp.where(valid, jnp.minimum(xg, 0.0) - jnp.log1p(jnp.exp(-jnp.abs(xg))), 0.0)
        l1, l2, l3 = _split3(logf)
        cs = carry + (_dot(tri, l1) + _dot(tri, l2) + _dot(tri, l3))
        ccol_ref[i * blk:(i + 1) * blk, :] = cs
        s1, s2, s3 = _split3(cs)
        crow_ref[0, :, i * blk:(i + 1) * blk] = _nt_dot(sel, s1) + _nt_dot(sel, s2) + _nt_dot(sel, s3)
        carry = cs[blk - 1:blk, :]


def _cumgate(sm, fb, c0, *, batch, seq):
    blk = min(256, seq)
    body = functools.partial(_cumgate_body, seq=seq, blk=blk)
    return pl.pallas_call(
        body,
        grid=(batch,),
        in_specs=[pl.BlockSpec((seq, LANES), lambda b: (b, 0)), pl.BlockSpec((1, LANES), lambda b: (0, 0)),
                  pl.BlockSpec((1, LANES), lambda b: (0, 0))],
        out_specs=[pl.BlockSpec((seq, LANES), lambda b: (b, 0)),
                   pl.BlockSpec((1, SUBLANES, seq), lambda b: (b, 0, 0))],
        out_shape=[jax.ShapeDtypeStruct((batch * seq, LANES), F32),
                   jax.ShapeDtypeStruct((batch, SUBLANES, seq), F32)],
        compiler_params=pltpu.CompilerParams(dimension_semantics=("arbitrary",), vmem_limit_bytes=VMEM_LIMIT),
        name="cumgate",
    )(sm, fb, c0)


def _fox_body(q_ref, k_ref, v_ref, km_ref, vm_ref, ccol_ref, crow_ref, crowm_ref, o_ref, *, tq, tk):
    pair = pl.program_id(1)
    i = pl.program_id(2)
    lane = lax.broadcasted_iota(jnp.int32, (1, LANES), 1)
    q_all = q_ref[...]
    ccol = ccol_ref[...]
    n_full = (i * tq) // tk
    n_diag = tq // tk
    rowpos = i * tq + lax.broadcasted_iota(jnp.int32, (tq, tk), 0)
    colpos = lax.broadcasted_iota(jnp.int32, (tq, tk), 1)
    outs = []
    for hh in range(2):
        head = 2 * pair + hh
        hmask = (lane >= hh * FOX_DH) & (lane < (hh + 1) * FOX_DH)
        q = jnp.where(hmask, q_all, jnp.zeros_like(q_all))
        cq = jnp.sum(jnp.where(lane == LANE_FORGET + head, ccol, 0.0), axis=-1, keepdims=True)

        def step(s, vblk, m, l, acc):
            m_new = jnp.maximum(m, jnp.max(s, axis=-1, keepdims=True))
            p = jnp.exp(s - m_new)
            alpha = jnp.exp(m - m_new)
            l = alpha * l + jnp.sum(p, axis=-1, keepdims=True)
            acc = alpha * acc + _dot(p.astype(BF16), vblk)
            return m_new, l, acc

        ckm = crowm_ref[pl.ds(head, 1), :]
        s = _nt_dot(q, km_ref[...]) + (cq - ckm)
        s = jnp.where(lax.broadcasted_iota(jnp.int32, (tq, LANES), 1) < N_META, s, MASK_VALUE)
        m0 = jnp.full((tq, 1), MASK_VALUE, F32)
        l0 = jnp.zeros((tq, 1), F32)
        a0 = jnp.zeros((tq, LANES), F32)
        m, l, acc = step(s, vm_ref[...], m0, l0, a0)

        def full_block(jb, carry):
            start = pl.multiple_of(jb * tk, tk)
            ck = crow_ref[0, pl.ds(head, 1), pl.ds(start, tk)]
            s = _nt_dot(q, k_ref[pl.ds(start, tk), :]) + (cq - ck)
            return step(s, v_ref[pl.ds(start, tk), :], *carry)

        m, l, acc = lax.fori_loop(0, n_full, full_block, (m, l, acc))
        for d in range(n_diag):
            start = pl.multiple_of(i * tq + d * tk, tk)
            ck = crow_ref[0, pl.ds(head, 1), pl.ds(start, tk)]
            s = _nt_dot(q, k_ref[pl.ds(start, tk), :]) + (cq - ck)
            s = jnp.where(rowpos >= colpos + start, s, MASK_VALUE)
            m, l, acc = step(s, v_ref[pl.ds(start, tk), :], m, l, acc)
        outs.append(acc / l)
    o_ref[...] = jnp.where(lane < FOX_DH, outs[0], outs[1]).astype(o_ref.dtype)


def _fox(fq, fk, fv, km, vm, ccol, crow, crowm, *, batch, seq, tq, tk):
    nq = seq // tq
    npair = FOX_HEADS // 2
    body = functools.partial(_fox_body, tq=tq, tk=tk)
    return pl.pallas_call(
        body,
        grid=(batch, npair, nq),
        in_specs=[pl.BlockSpec((tq, LANES), lambda b, p, i: (b * nq + i, p)),
                  pl.BlockSpec((seq, LANES), lambda b, p, i: (b, p)),
                  pl.BlockSpec((seq, LANES), lambda b, p, i: (b, p)),
                  pl.BlockSpec((LANES, LANES), lambda b, p, i: (0, p)),
                  pl.BlockSpec((LANES, LANES), lambda b, p, i: (0, p)),
                  pl.BlockSpec((tq, LANES), lambda b, p, i: (b * nq + i, 0)),
                  pl.BlockSpec((1, SUBLANES, seq), lambda b, p, i: (b, 0, 0)),
                  pl.BlockSpec((SUBLANES, LANES), lambda b, p, i: (0, 0))],
        out_specs=pl.BlockSpec((tq, LANES), lambda b, p, i: (b * nq + i, p)),
        out_shape=jax.ShapeDtypeStruct((batch * seq, FOX_HD), BF16),
        compiler_params=pltpu.CompilerParams(dimension_semantics=("arbitrary", "arbitrary", "arbitrary"),
                                             vmem_limit_bytes=VMEM_LIMIT),
        name="fox",
    )(fq, fk, fv, km, vm, ccol, crow, crowm)


def _ffn_body(x_ref, og_ref, of_ref, wo_ref, nw_ref, wg_ref, wu_ref, wd_ref, fw_ref, out_ref, act_ref, *, fchunk):
    h1 = x_ref[...] + _dot(og_ref[...], wo_ref[0:GDN_HD, :]) + _dot(of_ref[...], wo_ref[GDN_HD:GDN_HD + FOX_HD, :])
    n = (h1 * lax.rsqrt(jnp.mean(h1 * h1, axis=-1, keepdims=True) + EPS) * nw_ref[...]).astype(BF16)
    d_ff = wg_ref.shape[1]
    for c in range(d_ff // fchunk):
        sl = slice(c * fchunk, (c + 1) * fchunk)
        g = _dot(n, wg_ref[:, sl])
        up = _dot(n, wu_ref[:, sl])
        act_ref[:, sl] = (_silu(g) * up).astype(BF16)
    acc = h1 + _dot(act_ref[...], wd_ref[...])
    out_ref[...] = acc * lax.rsqrt(jnp.mean(acc * acc, axis=-1, keepdims=True) + EPS) * fw_ref[...]


def _ffn(x2d, og, of, wo, nw, wg, wu, wd, fw, tm):
    rows = x2d.shape[0]
    row_spec = lambda n: pl.BlockSpec((tm, n), lambda i: (i, 0))
    const_spec = lambda a: pl.BlockSpec(a.shape, lambda i: (0, 0), pipeline_mode=pl.Buffered(1))
    body = functools.partial(_ffn_body, fchunk=256)
    return pl.pallas_call(
        body,
        grid=(rows // tm,),
        in_specs=[row_spec(D_MODEL), row_spec(GDN_HD), row_spec(FOX_HD), const_spec(wo), const_spec(nw),
                  const_spec(wg), const_spec(wu), const_spec(wd), const_spec(fw)],
        out_specs=row_spec(D_MODEL),
        out_shape=jax.ShapeDtypeStruct((rows, D_MODEL), F32),
        scratch_shapes=[pltpu.VMEM((tm, wg.shape[1]), BF16)],
        compiler_params=pltpu.CompilerParams(dimension_semantics=("arbitrary",), vmem_limit_bytes=VMEM_LIMIT),
        name="ffn",
    )(x2d, og, of, wo, nw, wg, wu, wd, fw)


def _lane_vec(vals, offset):
    return jnp.zeros((1, LANES), F32).at[0, offset:offset + vals.shape[0]].set(vals.astype(F32))


def kernel(x, meta_tokens, attn_norm_w, w_in, conv_w, a_log, dt_bias, gdn_norm_w, fgate_b, w_out, ffn_norm_w,
           w_gate, w_up, w_down, final_norm_w):
    batch, seq, _ = x.shape
    depth = w_in.shape[0]
    assert depth == 1 and seq % 512 == 0
    w = w_in[0]
    s = [0, 512, 1024, 1536, 2048, 2052, 2056, 2568, 3080, 3592, 3600]
    wbig = jnp.concatenate([w[:, s[0]:s[4]], w[:, s[6]:s[9]]], axis=1).astype(BF16)
    wsm = jnp.concatenate([w[:, s[4]:s[6]], w[:, s[9]:s[10]],
                           jnp.zeros((D_MODEL, LANES - 2 * GDN_HEADS - FOX_HEADS), F32)], axis=1).astype(BF16)
    anw = attn_norm_w[0][None, :]
    avec = _lane_vec(-jnp.exp(a_log[0]), LANE_DECAY)
    dtvec = _lane_vec(dt_bias[0], LANE_DECAY)
    fbvec = _lane_vec(fgate_b[0], LANE_FORGET)
    gnw = gdn_norm_w[0][None, :]
    convw = conv_w[0]

    x2d = x.reshape(batch * seq, D_MODEL)

    qkv_m, z_m, _, fk_m, fv_m, sm_m = _inproj(meta_tokens, anw, wbig, wsm, tm=N_META)
    zeros_hist = jnp.zeros((SUBLANES, GDN_QKV), F32)
    zeros_state = jnp.zeros((GDN_HEADS, GDN_D, GDN_D), F32)
    _, s_meta = _gdn(qkv_m, z_m, sm_m, convw, zeros_hist, zeros_state, avec, dtvec, gnw,
                     batch=1, seq=N_META, rows=N_META, chunk=N_META)
    ccol_m, crow_m = _cumgate(sm_m, fbvec, jnp.zeros((1, LANES), F32), batch=1, seq=N_META)
    pad_rows = LANES - N_META
    km = jnp.pad(fk_m, ((0, pad_rows), (0, 0)))
    vm = jnp.pad(fv_m, ((0, pad_rows), (0, 0)))
    crowm = jnp.pad(crow_m[0], ((0, 0), (0, pad_rows)))

    qkv, z, fq, fk, fv, sm = _inproj(x2d, anw, wbig, wsm, tm=512)
    o_gdn, _ = _gdn(qkv, z, sm, convw, qkv_m[N_META - SUBLANES:], s_meta[0], avec, dtvec, gnw,
                    batch=batch, seq=seq, rows=256, chunk=CHUNK)
    ccol, crow = _cumgate(sm, fbvec, ccol_m[N_META - 1:], batch=batch, seq=seq)
    o_fox = _fox(fq, fk, fv, km, vm, ccol, crow, crowm, batch=batch, seq=seq, tq=256, tk=256)

    out = _ffn(x2d, o_gdn, o_fox, w_out[0].astype(BF16), ffn_norm_w[0][None, :], w_gate[0].astype(BF16),
               w_up[0].astype(BF16), w_down[0].astype(BF16), final_norm_w[None, :], tm=512)
    return out.reshape(batch, seq, D_MODEL)
```

```python
import functools
import math

import numpy as np
import jax
import jax.numpy as jnp
from jax import lax
from jax.experimental import pallas as pl
from jax.experimental.pallas import tpu as pltpu

F32 = jnp.float32
BF16 = jnp.bfloat16

D_MODEL = 1024
N_META = 16
GDN_HEADS = 4
GDN_D = 128
FOX_HEADS = 8
FOX_DH = 64
CONV_K = 4
EPS = 1e-6
MASK_VALUE = -1e30
LOG2E = 1.4426950408889634
GDN_QKV = 3 * GDN_HEADS * GDN_D
GDN_HD = GDN_HEADS * GDN_D
FOX_HD = FOX_HEADS * FOX_DH
LANES = 128
SUBLANES = 8
LANE_BETA = 0
LANE_DECAY = GDN_HEADS
LANE_FORGET = 2 * GDN_HEADS
BIAS_STRIDE = 8
INV_BASE = 16
VMEM_LIMIT = 56 * 1024 * 1024


def _nt_dot(a, b):
    return lax.dot_general(a, b, (((1,), (1,)), ((), ())), preferred_element_type=F32)


def _tn_dot(a, b):
    return lax.dot_general(a, b, (((0,), (0,)), ((), ())), preferred_element_type=F32)


def _dot(a, b):
    return jnp.dot(a, b, preferred_element_type=F32)


def _split3(x):
    x1 = x.astype(BF16)
    r1 = x - x1.astype(F32)
    x2 = r1.astype(BF16)
    x3 = (r1 - x2.astype(F32)).astype(BF16)
    return x1, x2, x3


def _softplus(x):
    return jnp.maximum(x, 0.0) + jnp.log1p(jnp.exp(-jnp.abs(x)))


def _sigmoid(x):
    return 1.0 / (1.0 + jnp.exp(-x))


def _silu(x):
    return x * _sigmoid(x)


def _inproj_body(x_ref, nw_ref, wbig_ref, wsm_ref, qkv_ref, z_ref, fq_ref, fk_ref, fv_ref, sm_ref):
    x = x_ref[...]
    u = (x * lax.rsqrt(jnp.mean(x * x, axis=-1, keepdims=True) + EPS) * nw_ref[...]).astype(BF16)
    o0 = GDN_QKV
    o1 = o0 + GDN_HD
    o2 = o1 + FOX_HD
    o3 = o2 + FOX_HD
    o4 = o3 + FOX_HD
    qkv_ref[...] = _dot(u, wbig_ref[:, 0:o0])
    z_ref[...] = _dot(u, wbig_ref[:, o0:o1])
    fq_ref[...] = (_dot(u, wbig_ref[:, o1:o2]) * (LOG2E * FOX_DH ** -0.5)).astype(BF16)
    fk_ref[...] = _dot(u, wbig_ref[:, o2:o3]).astype(BF16)
    fv_ref[...] = _dot(u, wbig_ref[:, o3:o4]).astype(BF16)
    sm_ref[...] = _dot(u, wsm_ref[...])


def _inproj(x2d, nw, wbig, wsm, tm):
    rows = x2d.shape[0]
    row_spec = lambda n: pl.BlockSpec((tm, n), lambda i: (i, 0))
    const_spec = lambda a: pl.BlockSpec(a.shape, lambda i: (0, 0))
    return pl.pallas_call(
        _inproj_body,
        grid=(rows // tm,),
        in_specs=[row_spec(D_MODEL), const_spec(nw), const_spec(wbig), const_spec(wsm)],
        out_specs=[row_spec(GDN_QKV), row_spec(GDN_HD), row_spec(FOX_HD), row_spec(FOX_HD), row_spec(FOX_HD),
                   row_spec(LANES)],
        out_shape=[jax.ShapeDtypeStruct((rows, GDN_QKV), F32), jax.ShapeDtypeStruct((rows, GDN_HD), F32),
                   jax.ShapeDtypeStruct((rows, FOX_HD), BF16), jax.ShapeDtypeStruct((rows, FOX_HD), BF16),
                   jax.ShapeDtypeStruct((rows, FOX_HD), BF16), jax.ShapeDtypeStruct((rows, LANES), F32)],
        compiler_params=pltpu.CompilerParams(dimension_semantics=("arbitrary",), vmem_limit_bytes=VMEM_LIMIT),
        name="inproj",
    )(x2d, nw, wbig, wsm)


def _unit_lower_inverses(a_mats, r_i, c_i, size):
    n = range(len(a_mats))
    base = min(INV_BASE, size)
    eye = jnp.where(r_i == c_i, 1.0, 0.0).astype(F32)
    rc_xor = r_i ^ c_i
    p_pow = [jnp.where(rc_xor < base, -a_mats[i], 0.0) for i in n]
    t_inv = [eye + p_pow[i] for i in n]
    for _ in range(int(math.log2(base)) - 1):
        p16 = [p_pow[i].astype(BF16) for i in n]
        p_pow = [_dot(p16[i], p16[i]) for i in n]
        t_inv = [t_inv[i] + _dot(t_inv[i].astype(BF16), p_pow[i].astype(BF16)) for i in n]
    half = base
    while half < size:
        lower_left = (rc_xor >= half) & (rc_xor < 2 * half)
        t16 = [t_inv[i].astype(BF16) for i in n]
        y16 = [_dot(jnp.where(lower_left, a_mats[i], 0.0).astype(BF16), t16[i]).astype(BF16) for i in n]
        t_inv = [t_inv[i] - _dot(t16[i], y16[i]) for i in n]
        half *= 2
    return t_inv


def _gdn_body(qkv_ref, z_ref, sm_ref, convw_ref, hist_ref, s0_ref, avec_ref, dtvec_ref, normw_ref,
              o_ref, sout_ref, state_ref, xpad_ref, *, rows):
    j = pl.program_id(1)

    @pl.when(j == 0)
    def _():
        state_ref[...] = s0_ref[...]
        xpad_ref[0:SUBLANES, :] = hist_ref[...]

    xpad_ref[SUBLANES:SUBLANES + rows, :] = qkv_ref[...]
    conv = convw_ref[0:1, :] * xpad_ref[SUBLANES - 3:SUBLANES - 3 + rows, :]
    for kk in range(1, CONV_K):
        conv = conv + convw_ref[kk:kk + 1, :] * xpad_ref[SUBLANES - 3 + kk:SUBLANES - 3 + kk + rows, :]
    xpad_ref[0:SUBLANES, :] = xpad_ref[rows:rows + SUBLANES, :]
    act = _silu(conv)

    sm = sm_ref[...]
    beta_all = _sigmoid(sm)
    g_all = avec_ref[...] * _softplus(sm + dtvec_ref[...])

    r_i = lax.broadcasted_iota(jnp.int32, (rows, rows), 0)
    c_i = lax.broadcasted_iota(jnp.int32, (rows, rows), 1)
    incl = r_i >= c_i
    strict = r_i > c_i
    tri = jnp.where(incl, 1.0, 0.0).astype(BF16)
    g1, g2, g3 = _split3(g_all)
    gc_all = _dot(tri, g1) + _dot(tri, g2) + _dot(tri, g3)
    sel = jnp.where(lax.broadcasted_iota(jnp.int32, (SUBLANES, LANES), 1)
                    == lax.broadcasted_iota(jnp.int32, (SUBLANES, LANES), 0) + LANE_DECAY, 1.0, 0.0).astype(BF16)
    c1, c2, c3 = _split3(gc_all)
    gc_rows = _nt_dot(sel, c1) + _nt_dot(sel, c2) + _nt_dot(sel, c3)
    egc_all = jnp.exp(gc_all)
    gl_row = gc_all[rows - 1:rows, :]
    ekg_all = jnp.exp(gl_row - gc_all)
    egl_all = jnp.exp(gl_row)
    normw = normw_ref[...]

    heads = range(GDN_HEADS)
    qs, ks, kbs, k16s, rhss, decays, egcs, ekgs, egls = [], [], [], [], [], [], [], [], []
    for h in heads:
        q = act[:, h * GDN_D:(h + 1) * GDN_D]
        k = act[:, GDN_HD + h * GDN_D:GDN_HD + (h + 1) * GDN_D]
        v = act[:, 2 * GDN_HD + h * GDN_D:2 * GDN_HD + (h + 1) * GDN_D]
        q = q * lax.rsqrt(jnp.sum(q * q, axis=-1, keepdims=True) + EPS) * (GDN_D ** -0.5)
        k = k * lax.rsqrt(jnp.sum(k * k, axis=-1, keepdims=True) + EPS)
        beta = beta_all[:, LANE_BETA + h:LANE_BETA + h + 1]
        gcol = gc_all[:, LANE_DECAY + h:LANE_DECAY + h + 1]
        egc = egc_all[:, LANE_DECAY + h:LANE_DECAY + h + 1]
        grow = gc_rows[h:h + 1, :]
        kb = k * beta
        qs.append(q)
        ks.append(k)
        kbs.append(kb)
        k16s.append(k.astype(BF16))
        egcs.append(egc)
        ekgs.append(ekg_all[:, LANE_DECAY + h:LANE_DECAY + h + 1])
        egls.append(egl_all[:, LANE_DECAY + h:LANE_DECAY + h + 1])
        decays.append(jnp.where(incl, jnp.exp(jnp.where(incl, gcol - grow, 0.0)), 0.0))
        rhss.append(jnp.concatenate([v * beta, kb * egc], axis=1).astype(BF16))
    a_mats = [jnp.where(strict, _nt_dot(kbs[h].astype(BF16), k16s[h]) * decays[h], 0.0) for h in heads]
    qks = [jnp.where(incl, _nt_dot(qs[h].astype(BF16), k16s[h]) * decays[h], 0.0).astype(BF16) for h in heads]
    t_invs = _unit_lower_inverses(a_mats, r_i, c_i, rows)
    uws = [_dot(t_invs[h].astype(BF16), rhss[h]) for h in heads]
    s_old = [state_ref[h] for h in heads]
    wss = [_dot(jnp.concatenate([uws[h][:, GDN_D:], qs[h] * egcs[h]], axis=0).astype(BF16), s_old[h].astype(BF16))
           for h in heads]
    v_news = [(uws[h][:, :GDN_D] - wss[h][:rows]).astype(BF16) for h in heads]
    outs = [wss[h][rows:] + _dot(qks[h], v_news[h]) for h in heads]
    for h in heads:
        state_ref[h] = s_old[h] * egls[h] + _tn_dot((ks[h] * ekgs[h]).astype(BF16), v_news[h])
    for h in heads:
        zg = _silu(z_ref[:, h * GDN_D:(h + 1) * GDN_D])
        o = outs[h]
        o = o * lax.rsqrt(jnp.mean(o * o, axis=-1, keepdims=True) + EPS) * normw * zg
        o_ref[:, h * GDN_D:(h + 1) * GDN_D] = o.astype(o_ref.dtype)

    @pl.when(j == pl.num_programs(1) - 1)
    def _():
        sout_ref[0] = state_ref[...]


def _gdn(qkv, z, sm, convw, hist, s0, avec, dtvec, normw, *, batch, seq, rows):
    nblk = seq // rows
    row_spec = lambda n: pl.BlockSpec((rows, n), lambda b, j: (b * nblk + j, 0))
    const2 = lambda a: pl.BlockSpec(a.shape, lambda b, j: (0, 0))
    body = functools.partial(_gdn_body, rows=rows)
    return pl.pallas_call(
        body,
        grid=(batch, nblk),
        in_specs=[row_spec(GDN_QKV), row_spec(GDN_HD), row_spec(LANES), const2(convw), const2(hist),
                  pl.BlockSpec(s0.shape, lambda b, j: (0, 0, 0)), const2(avec), const2(dtvec), const2(normw)],
        out_specs=[row_spec(GDN_HD), pl.BlockSpec((1, GDN_HEADS, GDN_D, GDN_D), lambda b, j: (b, 0, 0, 0))],
        out_shape=[jax.ShapeDtypeStruct((batch * seq, GDN_HD), BF16),
                   jax.ShapeDtypeStruct((batch, GDN_HEADS, GDN_D, GDN_D), F32)],
        scratch_shapes=[pltpu.VMEM((GDN_HEADS, GDN_D, GDN_D), F32),
                        pltpu.VMEM((rows + SUBLANES, GDN_QKV), F32)],
        compiler_params=pltpu.CompilerParams(dimension_semantics=("arbitrary", "arbitrary"),
                                             vmem_limit_bytes=VMEM_LIMIT),
        name="gdn",
    )(qkv, z, sm, convw, hist, s0, avec, dtvec, normw)


def _bias_constants():
    mq = np.zeros((3, LANES, FOX_HD), np.float32)
    mk = np.zeros((3, LANES, FOX_HD), np.float32)
    oq = np.zeros((1, FOX_HD), np.float32)
    ok = np.zeros((1, FOX_HD), np.float32)
    for head in range(FOX_HEADS):
        base = (head // 2) * LANES + BIAS_STRIDE * (head % 2)
        for t in range(3):
            mq[t, LANE_FORGET + head, base + t] = 1.0
            mk[t, LANE_FORGET + head, base + 3 + t] = -1.0
            oq[0, base + 3 + t] = 1.0
            ok[0, base + t] = 1.0
    return jnp.asarray(mq, BF16), jnp.asarray(mk, BF16), jnp.asarray(oq), jnp.asarray(ok)


def _cumgate_body(sm_ref, fb_ref, c0_ref, mq_ref, mk_ref, oq_ref, ok_ref, ccol_ref, qb_ref, kb_ref, *, seq, blk):
    lane = lax.broadcasted_iota(jnp.int32, (1, LANES), 1)
    valid = (lane >= LANE_FORGET) & (lane < LANE_FORGET + FOX_HEADS)
    r_i = lax.broadcasted_iota(jnp.int32, (blk, blk), 0)
    c_i = lax.broadcasted_iota(jnp.int32, (blk, blk), 1)
    tri = jnp.where(r_i >= c_i, 1.0, 0.0).astype(BF16)
    carry = c0_ref[...]
    for i in range(seq // blk):
        xg = sm_ref[i * blk:(i + 1) * blk, :] + fb_ref[...]
        logf = jnp.where(valid, jnp.minimum(xg, 0.0) - jnp.log1p(jnp.exp(-jnp.abs(xg))), 0.0)
        l1, l2, l3 = _split3(logf)
        cs = carry + (_dot(tri, l1) + _dot(tri, l2) + _dot(tri, l3))
        ccol_ref[i * blk:(i + 1) * blk, :] = cs
        s1, s2, s3 = _split3(cs * LOG2E)
        qb = _dot(s1, mq_ref[0]) + _dot(s2, mq_ref[1]) + _dot(s3, mq_ref[2]) + oq_ref[...]
        kb = _dot(s1, mk_ref[0]) + _dot(s2, mk_ref[1]) + _dot(s3, mk_ref[2]) + ok_ref[...]
        qb_ref[i * blk:(i + 1) * blk, :] = qb.astype(BF16)
        kb_ref[i * blk:(i + 1) * blk, :] = kb.astype(BF16)
        carry = cs[blk - 1:blk, :]


def _cumgate(sm, fb, c0, consts, *, batch, seq):
    blk = min(256, seq)
    mq, mk, oq, ok = consts
    body = functools.partial(_cumgate_body, seq=seq, blk=blk)
    c2 = lambda a: pl.BlockSpec(a.shape, lambda b: (0, 0))
    c3 = lambda a: pl.BlockSpec(a.shape, lambda b: (0, 0, 0))
    return pl.pallas_call(
        body,
        grid=(batch,),
        in_specs=[pl.BlockSpec((seq, LANES), lambda b: (b, 0)), c2(fb), c2(c0), c3(mq), c3(mk), c2(oq), c2(ok)],
        out_specs=[pl.BlockSpec((seq, LANES), lambda b: (b, 0)), pl.BlockSpec((seq, FOX_HD), lambda b: (b, 0)),
                   pl.BlockSpec((seq, FOX_HD), lambda b: (b, 0))],
        out_shape=[jax.ShapeDtypeStruct((batch * seq, LANES), F32),
                   jax.ShapeDtypeStruct((batch * seq, FOX_HD), BF16),
                   jax.ShapeDtypeStruct((batch * seq, FOX_HD), BF16)],
        compiler_params=pltpu.CompilerParams(dimension_semantics=("arbitrary",), vmem_limit_bytes=VMEM_LIMIT),
        name="cumgate",
    )(sm, fb, c0, mq, mk, oq, ok)


def _fox_body(q_ref, qb_ref, k_ref, kb_ref, v_ref, km_ref, kbm_ref, vm_ref, o_ref, s_scr, p_scr, *, tq):
    i = pl.program_id(2)
    wide = 2 * tq
    lane = lax.broadcasted_iota(jnp.int32, (1, LANES), 1)
    q_all = q_ref[...]
    qb_all = qb_ref[...]
    zero = jnp.zeros_like(q_all)
    one = jnp.ones_like(q_all[0:1, :])
    own = [lane < FOX_DH, lane >= FOX_DH]
    q_aug = []
    for hh in range(2):
        bias_lanes = (lane >= BIAS_STRIDE * hh) & (lane < BIAS_STRIDE * (hh + 1))
        q_aug.append(jnp.concatenate([jnp.where(own[hh], q_all, zero), jnp.where(bias_lanes, qb_all, zero)], axis=1))

    def keys(start, width):
        return jnp.concatenate([k_ref[pl.ds(start, width), :], kb_ref[pl.ds(start, width), :]], axis=1)

    def values(start, width, hh):
        return jnp.where(own[hh], v_ref[pl.ds(start, width), :], one)

    n_full = i // 2
    two = range(2)

    diag_start = pl.multiple_of(n_full * wide, wide)
    meta_mask = lax.broadcasted_iota(jnp.int32, (tq, LANES), 1) < N_META
    causal = (lax.broadcasted_iota(jnp.int32, (tq, wide), 1)
              <= lax.broadcasted_iota(jnp.int32, (tq, wide), 0) + (i % 2) * tq)
    km_aug = jnp.concatenate([km_ref[...], kbm_ref[...]], axis=1)
    kd_aug = keys(diag_start, wide)
    s_meta = [jnp.where(meta_mask, _nt_dot(q_aug[hh], km_aug), MASK_VALUE) for hh in two]
    s_diag = [jnp.where(causal, _nt_dot(q_aug[hh], kd_aug), MASK_VALUE) for hh in two]

    n_pairs = (n_full + 1) // 2

    def block_start(t):
        return pl.multiple_of(jnp.clip(t, 0, jnp.maximum(n_full - 1, 0)) * wide, wide)

    def pipeline_step(u, carry):
        return one_step(2 * u + 1, 1, 2 * u + 1 < n_full, one_step(2 * u, 0, None, carry))

    def one_step(t, cur, real_block, carry):
        alpha_prev, m, acc = carry
        nxt = 1 - cur
        prev_start = block_start(t - 1)
        next_aug = keys(block_start(t + 1), wide)
        acc = tuple(alpha_prev[hh] * acc[hh] + _dot(p_scr[nxt, hh], values(prev_start, wide, hh)) for hh in two)
        m_new = tuple(jnp.maximum(m[hh], jnp.max(s_scr[cur, hh], axis=-1, keepdims=True)) for hh in two)
        if real_block is not None:
            m_new = tuple(jnp.where(real_block, m_new[hh], m[hh]) for hh in two)
        alpha = tuple(jnp.exp2(m[hh] - m_new[hh]) for hh in two)
        for hh in two:
            p_scr[cur, hh] = jnp.exp2(s_scr[cur, hh] - m_new[hh]).astype(BF16)
        for hh in two:
            s_scr[nxt, hh] = _nt_dot(q_aug[hh], next_aug)
        return alpha, m_new, acc

    first_aug = keys(block_start(0), wide)
    for hh in two:
        s_scr[0, hh] = _nt_dot(q_aug[hh], first_aug)
        p_scr[1, hh] = jnp.zeros((tq, wide), BF16)
    carry = (tuple(jnp.ones((tq, 1), F32) for _ in two),
             tuple(jnp.full((tq, 1), MASK_VALUE, F32) for _ in two),
             tuple(jnp.zeros((tq, LANES), F32) for _ in two))
    alpha_last, m_b, acc_b = lax.fori_loop(0, n_pairs, pipeline_step, carry)
    t_last = 2 * n_pairs - 1
    last_start = block_start(t_last)
    keep_last = jnp.where(t_last < n_full, 1.0, 0.0)

    m_a = [jnp.maximum(jnp.max(s_meta[hh], axis=-1, keepdims=True), jnp.max(s_diag[hh], axis=-1, keepdims=True))
           for hh in two]
    acc_a = [_dot(jnp.exp2(s_meta[hh] - m_a[hh]).astype(BF16), jnp.where(own[hh], vm_ref[...], one))
             + _dot(jnp.exp2(s_diag[hh] - m_a[hh]).astype(BF16), values(diag_start, wide, hh)) for hh in two]

    outs = []
    for hh in two:
        acc_bh = alpha_last[hh] * acc_b[hh] + keep_last * _dot(p_scr[1, hh], values(last_start, wide, hh))
        m = jnp.maximum(m_a[hh], m_b[hh])
        acc = jnp.exp2(m_a[hh] - m) * acc_a[hh] + jnp.exp2(m_b[hh] - m) * acc_bh
        outs.append(acc / pltpu.roll(acc, FOX_DH, 1))
    o_ref[...] = jnp.where(own[0], outs[0], outs[1]).astype(o_ref.dtype)


def _fox(fq, qb, fk, kb, fv, km, kbm, vm, *, batch, seq, tq):
    nq = seq // tq
    npair = FOX_HEADS // 2
    body = functools.partial(_fox_body, tq=tq)
    qspec = pl.BlockSpec((tq, LANES), lambda b, p, i: (b * nq + i, p))
    kspec = pl.BlockSpec((seq, LANES), lambda b, p, i: (b, p))
    mspec = pl.BlockSpec((LANES, LANES), lambda b, p, i: (0, p))
    return pl.pallas_call(
        body,
        grid=(batch, npair, nq),
        in_specs=[qspec, qspec, kspec, kspec, kspec, mspec, mspec, mspec],
        out_specs=qspec,
        out_shape=jax.ShapeDtypeStruct((batch * seq, FOX_HD), BF16),
        scratch_shapes=[pltpu.VMEM((2, 2, tq, 2 * tq), F32), pltpu.VMEM((2, 2, tq, 2 * tq), BF16)],
        compiler_params=pltpu.CompilerParams(dimension_semantics=("arbitrary", "arbitrary", "arbitrary"),
                                             vmem_limit_bytes=VMEM_LIMIT),
        name="fox",
    )(fq, qb, fk, kb, fv, km, kbm, vm)


def _ffn_body(x_ref, og_ref, of_ref, wo_ref, nw_ref, wg_ref, wu_ref, wd_ref, fw_ref, out_ref, act_ref, *, fchunk):
    h1 = x_ref[...] + _dot(og_ref[...], wo_ref[0:GDN_HD, :]) + _dot(of_ref[...], wo_ref[GDN_HD:GDN_HD + FOX_HD, :])
    n = (h1 * lax.rsqrt(jnp.mean(h1 * h1, axis=-1, keepdims=True) + EPS) * nw_ref[...]).astype(BF16)
    d_ff = wg_ref.shape[1]
    for c in range(d_ff // fchunk):
        sl = slice(c * fchunk, (c + 1) * fchunk)
        g = _dot(n, wg_ref[:, sl])
        up = _dot(n, wu_ref[:, sl])
        act_ref[:, sl] = (_silu(g) * up).astype(BF16)
    acc = h1 + _dot(act_ref[...], wd_ref[...])
    out_ref[...] = acc * lax.rsqrt(jnp.mean(acc * acc, axis=-1, keepdims=True) + EPS) * fw_ref[...]


def _ffn(x2d, og, of, wo, nw, wg, wu, wd, fw, tm):
    rows = x2d.shape[0]
    row_spec = lambda n: pl.BlockSpec((tm, n), lambda i: (i, 0))
    const_spec = lambda a: pl.BlockSpec(a.shape, lambda i: (0, 0), pipeline_mode=pl.Buffered(1))
    body = functools.partial(_ffn_body, fchunk=256)
    return pl.pallas_call(
        body,
        grid=(rows // tm,),
        in_specs=[row_spec(D_MODEL), row_spec(GDN_HD), row_spec(FOX_HD), const_spec(wo), const_spec(nw),
                  const_spec(wg), const_spec(wu), const_spec(wd), const_spec(fw)],
        out_specs=row_spec(D_MODEL),
        out_shape=jax.ShapeDtypeStruct((rows, D_MODEL), F32),
        scratch_shapes=[pltpu.VMEM((tm, wg.shape[1]), BF16)],
        compiler_params=pltpu.CompilerParams(dimension_semantics=("arbitrary",), vmem_limit_bytes=VMEM_LIMIT),
        name="ffn",
    )(x2d, og, of, wo, nw, wg, wu, wd, fw)


def _lane_vec(vals, offset):
    return jnp.zeros((1, LANES), F32).at[0, offset:offset + vals.shape[0]].set(vals.astype(F32))


def kernel(x, meta_tokens, attn_norm_w, w_in, conv_w, a_log, dt_bias, gdn_norm_w, fgate_b, w_out, ffn_norm_w,
           w_gate, w_up, w_down, final_norm_w):
    batch, seq, _ = x.shape
    depth = w_in.shape[0]
    assert depth == 1 and seq % 512 == 0
    w = w_in[0]
    s = [0, 512, 1024, 1536, 2048, 2052, 2056, 2568, 3080, 3592, 3600]
    wbig = jnp.concatenate([w[:, s[0]:s[4]], w[:, s[6]:s[9]]], axis=1).astype(BF16)
    wsm = jnp.concatenate([w[:, s[4]:s[6]], w[:, s[9]:s[10]],
                           jnp.zeros((D_MODEL, LANES - 2 * GDN_HEADS - FOX_HEADS), F32)], axis=1).astype(BF16)
    anw = attn_norm_w[0][None, :]
    avec = _lane_vec(-jnp.exp(a_log[0]), LANE_DECAY)
    dtvec = _lane_vec(dt_bias[0], LANE_DECAY)
    fbvec = _lane_vec(fgate_b[0], LANE_FORGET)
    gnw = gdn_norm_w[0][None, :]
    convw = conv_w[0]
    bias_consts = _bias_constants()

    x2d = x.reshape(batch * seq, D_MODEL)

    qkv_m, z_m, _, fk_m, fv_m, sm_m = _inproj(meta_tokens, anw, wbig, wsm, tm=N_META)
    zeros_hist = jnp.zeros((SUBLANES, GDN_QKV), F32)
    zeros_state = jnp.zeros((GDN_HEADS, GDN_D, GDN_D), F32)
    _, s_meta = _gdn(qkv_m, z_m, sm_m, convw, zeros_hist, zeros_state, avec, dtvec, gnw,
                     batch=1, seq=N_META, rows=N_META)
    ccol_m, _, kb_m = _cumgate(sm_m, fbvec, jnp.zeros((1, LANES), F32), bias_consts, batch=1, seq=N_META)
    pad_rows = ((0, LANES - N_META), (0, 0))
    km = jnp.pad(fk_m, pad_rows)
    kbm = jnp.pad(kb_m, pad_rows)
    vm = jnp.pad(fv_m, pad_rows)

    qkv, z, fq, fk, fv, sm = _inproj(x2d, anw, wbig, wsm, tm=512)
    o_gdn, _ = _gdn(qkv, z, sm, convw, qkv_m[N_META - SUBLANES:], s_meta[0], avec, dtvec, gnw,
                    batch=batch, seq=seq, rows=256)
    _, qb, kb = _cumgate(sm, fbvec, ccol_m[N_META - 1:], bias_consts, batch=batch, seq=seq)
    o_fox = _fox(fq, qb, fk, kb, fv, km, kbm, vm, batch=batch, seq=seq, tq=256)

    out = _ffn(x2d, o_gdn, o_fox, w_out[0].astype(BF16), ffn_norm_w[0][None, :], w_gate[0].astype(BF16),
               w_up[0].astype(BF16), w_down[0].astype(BF16), final_norm_w[None, :], tm=512)
    return out.reshape(batch, seq, D_MODEL)
```

```python
import functools
import math

import numpy as np
import jax
import jax.numpy as jnp
from jax import lax
from jax.experimental import pallas as pl
from jax.experimental.pallas import tpu as pltpu

F32 = jnp.float32
BF16 = jnp.bfloat16

D_MODEL = 1024
N_META = 16
GDN_HEADS = 4
GDN_D = 128
FOX_HEADS = 8
FOX_DH = 64
CONV_K = 4
EPS = 1e-6
MASK_VALUE = -1e30
LOG2E = 1.4426950408889634
GDN_QKV = 3 * GDN_HEADS * GDN_D
GDN_HD = GDN_HEADS * GDN_D
FOX_HD = FOX_HEADS * FOX_DH
LANES = 128
SUBLANES = 8
LANE_BETA = 0
LANE_DECAY = GDN_HEADS
LANE_FORGET = 2 * GDN_HEADS
BIAS_STRIDE = 8
INV_BASE = 16
VMEM_LIMIT = 56 * 1024 * 1024


def _nt_dot(a, b):
    return lax.dot_general(a, b, (((1,), (1,)), ((), ())), preferred_element_type=F32)


def _tn_dot(a, b):
    return lax.dot_general(a, b, (((0,), (0,)), ((), ())), preferred_element_type=F32)


def _dot(a, b):
    return jnp.dot(a, b, preferred_element_type=F32)


def _split3(x):
    x1 = x.astype(BF16)
    r1 = x - x1.astype(F32)
    x2 = r1.astype(BF16)
    x3 = (r1 - x2.astype(F32)).astype(BF16)
    return x1, x2, x3


def _zero_after(x):
    bits = lax.bitcast_convert_type(x[0:SUBLANES, 0:LANES], jnp.uint32)
    zero = lax.shift_right_logical(lax.shift_right_logical(bits, jnp.uint32(16)), jnp.uint32(16))
    return lax.bitcast_convert_type(zero, F32)[0:1, :]


def _softplus(x):
    return jnp.maximum(x, 0.0) + jnp.log1p(jnp.exp(-jnp.abs(x)))


def _sigmoid(x):
    return 1.0 / (1.0 + jnp.exp(-x))


def _silu(x):
    return x * _sigmoid(x)


def _inproj_body(x_ref, nw_ref, wbig_ref, wsm_ref, qkv_ref, z_ref, fq_ref, fk_ref, fv_ref, sm_ref):
    x = x_ref[...]
    u = (x * lax.rsqrt(jnp.mean(x * x, axis=-1, keepdims=True) + EPS) * nw_ref[...]).astype(BF16)
    o0 = GDN_QKV
    o1 = o0 + GDN_HD
    o2 = o1 + FOX_HD
    o3 = o2 + FOX_HD
    o4 = o3 + FOX_HD
    qkv_ref[...] = _dot(u, wbig_ref[:, 0:o0])
    z_ref[...] = _dot(u, wbig_ref[:, o0:o1])
    fq_ref[...] = (_dot(u, wbig_ref[:, o1:o2]) * (LOG2E * FOX_DH ** -0.5)).astype(BF16)
    fk_ref[...] = _dot(u, wbig_ref[:, o2:o3]).astype(BF16)
    fv_ref[...] = _dot(u, wbig_ref[:, o3:o4]).astype(BF16)
    sm_ref[...] = _dot(u, wsm_ref[...])


def _inproj(x2d, nw, wbig, wsm, tm):
    rows = x2d.shape[0]
    row_spec = lambda n: pl.BlockSpec((tm, n), lambda i: (i, 0))
    const_spec = lambda a: pl.BlockSpec(a.shape, lambda i: (0, 0))
    return pl.pallas_call(
        _inproj_body,
        grid=(rows // tm,),
        in_specs=[row_spec(D_MODEL), const_spec(nw), const_spec(wbig), const_spec(wsm)],
        out_specs=[row_spec(GDN_QKV), row_spec(GDN_HD), row_spec(FOX_HD), row_spec(FOX_HD), row_spec(FOX_HD),
                   row_spec(LANES)],
        out_shape=[jax.ShapeDtypeStruct((rows, GDN_QKV), F32), jax.ShapeDtypeStruct((rows, GDN_HD), F32),
                   jax.ShapeDtypeStruct((rows, FOX_HD), BF16), jax.ShapeDtypeStruct((rows, FOX_HD), BF16),
                   jax.ShapeDtypeStruct((rows, FOX_HD), BF16), jax.ShapeDtypeStruct((rows, LANES), F32)],
        compiler_params=pltpu.CompilerParams(dimension_semantics=("arbitrary",), vmem_limit_bytes=VMEM_LIMIT),
        name="inproj",
    )(x2d, nw, wbig, wsm)


def _unit_lower_inverses(a_mats, r_i, c_i, size, fillers):
    fillers = iter(fillers)

    def fill(anchor):
        thunk = next(fillers, None)
        if thunk is not None:
            thunk(_zero_after(anchor))

    n = range(len(a_mats))
    base = min(INV_BASE, size)
    eye = jnp.where(r_i == c_i, 1.0, 0.0).astype(F32)
    rc_xor = r_i ^ c_i
    p_pow = [jnp.where(rc_xor < base, -a_mats[i], 0.0) for i in n]
    t_inv = [eye + p_pow[i] for i in n]
    for _ in range(int(math.log2(base)) - 1):
        p16 = [p_pow[i].astype(BF16) for i in n]
        p_pow = [_dot(p16[i], p16[i]) for i in n]
        fill(p_pow[0])
        t_upd = [_dot(t_inv[i].astype(BF16), p_pow[i].astype(BF16)) for i in n]
        fill(t_upd[0])
        t_inv = [t_inv[i] + t_upd[i] for i in n]
    half = base
    while half < size:
        lower_left = (rc_xor >= half) & (rc_xor < 2 * half)
        t16 = [t_inv[i].astype(BF16) for i in n]
        y = [_dot(jnp.where(lower_left, a_mats[i], 0.0).astype(BF16), t16[i]) for i in n]
        fill(y[0])
        t_upd = [_dot(t16[i], y[i].astype(BF16)) for i in n]
        fill(t_upd[0])
        t_inv = [t_inv[i] - t_upd[i] for i in n]
        half *= 2
    for thunk in fillers:
        thunk(_zero_after(t_inv[0]))
    return t_inv


def _gdn_body(qkv_ref, qkv_next_ref, z_ref, sm_ref, convw_ref, hist_ref, s0_ref, avec_ref, dtvec_ref, normw_ref,
              o_ref, sout_ref, state_ref, xpad_ref, act_ref, *, rows, lookahead):
    j = pl.program_id(1)
    cur = j % 2

    def conv_silu(lo, hi, zero_row=None):
        first = SUBLANES - (CONV_K - 1)
        w_first = convw_ref[0:1, lo:hi] if zero_row is None else convw_ref[0:1, lo:hi] + zero_row
        conv = w_first * xpad_ref[first:first + rows, lo:hi]
        for kk in range(1, CONV_K):
            conv = conv + convw_ref[kk:kk + 1, lo:hi] * xpad_ref[first + kk:first + kk + rows, lo:hi]
        return _silu(conv)

    def keep_history():
        xpad_ref[0:SUBLANES, :] = xpad_ref[rows:rows + SUBLANES, :]

    @pl.when(j == 0)
    def _():
        state_ref[...] = s0_ref[...]
        xpad_ref[0:SUBLANES, :] = hist_ref[...]
        xpad_ref[SUBLANES:SUBLANES + rows, :] = qkv_ref[...]
        act_ref[0] = conv_silu(0, GDN_QKV)
        keep_history()

    fillers = []
    if lookahead:
        def stage_next_block(zero_row):
            xpad_ref[SUBLANES:SUBLANES + rows, :] = qkv_next_ref[...]
        fillers.append(stage_next_block)
        for lo in range(0, GDN_QKV, GDN_D):
            def conv_slice(zero_row, lo=lo):
                act_ref[1 - cur, :, lo:lo + GDN_D] = conv_silu(lo, lo + GDN_D, zero_row)
            fillers.append(conv_slice)
        fillers.append(lambda zero_row: keep_history())

    sm = sm_ref[...]
    beta_all = _sigmoid(sm)
    g_all = avec_ref[...] * _softplus(sm + dtvec_ref[...])

    r_i = lax.broadcasted_iota(jnp.int32, (rows, rows), 0)
    c_i = lax.broadcasted_iota(jnp.int32, (rows, rows), 1)
    incl = r_i >= c_i
    strict = r_i > c_i
    tri = jnp.where(incl, 1.0, 0.0).astype(BF16)
    g1, g2, g3 = _split3(g_all)
    gc_all = _dot(tri, g1) + _dot(tri, g2) + _dot(tri, g3)
    sel = jnp.where(lax.broadcasted_iota(jnp.int32, (SUBLANES, LANES), 1)
                    == lax.broadcasted_iota(jnp.int32, (SUBLANES, LANES), 0) + LANE_DECAY, 1.0, 0.0).astype(BF16)
    c1, c2, c3 = _split3(gc_all)
    gc_rows = _nt_dot(sel, c1) + _nt_dot(sel, c2) + _nt_dot(sel, c3)
    egc_all = jnp.exp(gc_all)
    gl_row = gc_all[rows - 1:rows, :]
    ekg_all = jnp.exp(gl_row - gc_all)
    egl_all = jnp.exp(gl_row)
    normw = normw_ref[...]

    heads = range(GDN_HEADS)
    qs, ks, kbs, k16s, rhss, decays, egcs, ekgs, egls = [], [], [], [], [], [], [], [], []
    for h in heads:
        q = act_ref[cur, :, h * GDN_D:(h + 1) * GDN_D]
        k = act_ref[cur, :, GDN_HD + h * GDN_D:GDN_HD + (h + 1) * GDN_D]
        v = act_ref[cur, :, 2 * GDN_HD + h * GDN_D:2 * GDN_HD + (h + 1) * GDN_D]
        q = q * lax.rsqrt(jnp.sum(q * q, axis=-1, keepdims=True) + EPS) * (GDN_D ** -0.5)
        k = k * lax.rsqrt(jnp.sum(k * k, axis=-1, keepdims=True) + EPS)
        beta = beta_all[:, LANE_BETA + h:LANE_BETA + h + 1]
        gcol = gc_all[:, LANE_DECAY + h:LANE_DECAY + h + 1]
        egc = egc_all[:, LANE_DECAY + h:LANE_DECAY + h + 1]
        grow = gc_rows[h:h + 1, :]
        kb = k * beta
        qs.append(q)
        ks.append(k)
        kbs.append(kb)
        k16s.append(k.astype(BF16))
        egcs.append(egc)
        ekgs.append(ekg_all[:, LANE_DECAY + h:LANE_DECAY + h + 1])
        egls.append(egl_all[:, LANE_DECAY + h:LANE_DECAY + h + 1])
        decays.append(jnp.where(incl, jnp.exp(jnp.where(incl, gcol - grow, 0.0)), 0.0))
        rhss.append(jnp.concatenate([v * beta, kb * egc], axis=1).astype(BF16))
    a_mats = [jnp.where(strict, _nt_dot(kbs[h].astype(BF16), k16s[h]) * decays[h], 0.0) for h in heads]
    qks = [jnp.where(incl, _nt_dot(qs[h].astype(BF16), k16s[h]) * decays[h], 0.0).astype(BF16) for h in heads]
    t_invs = _unit_lower_inverses(a_mats, r_i, c_i, rows, fillers)
    uws = [_dot(t_invs[h].astype(BF16), rhss[h]) for h in heads]
    s_old = [state_ref[h] for h in heads]
    wss = [_dot(jnp.concatenate([uws[h][:, GDN_D:], qs[h] * egcs[h]], axis=0).astype(BF16), s_old[h].astype(BF16))
           for h in heads]
    v_news = [(uws[h][:, :GDN_D] - wss[h][:rows]).astype(BF16) for h in heads]
    outs = [wss[h][rows:] + _dot(qks[h], v_news[h]) for h in heads]
    for h in heads:
        state_ref[h] = s_old[h] * egls[h] + _tn_dot((ks[h] * ekgs[h]).astype(BF16), v_news[h])
    for h in heads:
        zg = _silu(z_ref[:, h * GDN_D:(h + 1) * GDN_D])
        o = outs[h]
        o = o * lax.rsqrt(jnp.mean(o * o, axis=-1, keepdims=True) + EPS) * normw * zg
        o_ref[:, h * GDN_D:(h + 1) * GDN_D] = o.astype(o_ref.dtype)

    @pl.when(j == pl.num_programs(1) - 1)
    def _():
        sout_ref[0] = state_ref[...]


def _gdn(qkv, z, sm, convw, hist, s0, avec, dtvec, normw, *, batch, seq, rows):
    nblk = seq // rows
    row_spec = lambda n: pl.BlockSpec((rows, n), lambda b, j: (b * nblk + j, 0))
    next_spec = pl.BlockSpec((rows, GDN_QKV), lambda b, j: (b * nblk + jnp.minimum(j + 1, nblk - 1), 0))
    const2 = lambda a: pl.BlockSpec(a.shape, lambda b, j: (0, 0))
    body = functools.partial(_gdn_body, rows=rows, lookahead=nblk > 1)
    return pl.pallas_call(
        body,
        grid=(batch, nblk),
        in_specs=[row_spec(GDN_QKV), next_spec, row_spec(GDN_HD), row_spec(LANES), const2(convw), const2(hist),
                  pl.BlockSpec(s0.shape, lambda b, j: (0, 0, 0)), const2(avec), const2(dtvec), const2(normw)],
        out_specs=[row_spec(GDN_HD), pl.BlockSpec((1, GDN_HEADS, GDN_D, GDN_D), lambda b, j: (b, 0, 0, 0))],
        out_shape=[jax.ShapeDtypeStruct((batch * seq, GDN_HD), BF16),
                   jax.ShapeDtypeStruct((batch, GDN_HEADS, GDN_D, GDN_D), F32)],
        scratch_shapes=[pltpu.VMEM((GDN_HEADS, GDN_D, GDN_D), F32),
                        pltpu.VMEM((rows + SUBLANES, GDN_QKV), F32),
                        pltpu.VMEM((2, rows, GDN_QKV), F32)],
        compiler_params=pltpu.CompilerParams(dimension_semantics=("arbitrary", "arbitrary"),
                                             vmem_limit_bytes=VMEM_LIMIT),
        name="gdn",
    )(qkv, qkv, z, sm, convw, hist, s0, avec, dtvec, normw)


def _bias_constants():
    sel = np.zeros((LANES, 2 * FOX_HD), np.float32)
    ones = np.zeros((1, 2 * FOX_HD), np.float32)
    for head in range(FOX_HEADS):
        base = (head // 2) * LANES + BIAS_STRIDE * (head % 2)
        for t in range(3):
            src = LANE_FORGET + t * FOX_HEADS + head
            sel[src, base + t] = 1.0
            sel[src, FOX_HD + base + 3 + t] = -1.0
            ones[0, base + 3 + t] = 1.0
            ones[0, FOX_HD + base + t] = 1.0
    return jnp.asarray(sel, BF16), jnp.asarray(ones)


def _pack3(x):
    hi = x.astype(BF16).astype(F32)
    rest = x - hi
    mid = rest.astype(BF16).astype(F32)
    lo = (rest - mid).astype(BF16).astype(F32)
    return (hi + pltpu.roll(mid, FOX_HEADS, 1) + pltpu.roll(lo, 2 * FOX_HEADS, 1)).astype(BF16)


def _cumgate_body(sm_ref, fb_ref, c0_ref, sel_ref, ones_ref, ccol_ref, qb_ref, kb_ref, *, seq, blk):
    lane = lax.broadcasted_iota(jnp.int32, (1, LANES), 1)
    valid = (lane >= LANE_FORGET) & (lane < LANE_FORGET + FOX_HEADS)
    r_i = lax.broadcasted_iota(jnp.int32, (blk, blk), 0)
    c_i = lax.broadcasted_iota(jnp.int32, (blk, blk), 1)
    tri = jnp.where(r_i >= c_i, 1.0, 0.0).astype(BF16)
    carry = c0_ref[...]
    for i in range(seq // blk):
        xg = sm_ref[i * blk:(i + 1) * blk, :] + fb_ref[...]
        logf = jnp.where(valid, jnp.minimum(xg, 0.0) - jnp.log1p(jnp.exp(-jnp.abs(xg))), 0.0)
        cs3 = _dot(tri, _pack3(logf))
        cs = cs3 + pltpu.roll(cs3, LANES - FOX_HEADS, 1) + pltpu.roll(cs3, LANES - 2 * FOX_HEADS, 1)
        cs = jnp.where(valid, carry + cs, 0.0)
        ccol_ref[i * blk:(i + 1) * blk, :] = cs
        bias = _dot(_pack3(cs * LOG2E), sel_ref[...]) + ones_ref[...]
        qb_ref[i * blk:(i + 1) * blk, :] = bias[:, :FOX_HD].astype(BF16)
        kb_ref[i * blk:(i + 1) * blk, :] = bias[:, FOX_HD:].astype(BF16)
        carry = cs[blk - 1:blk, :]


def _cumgate(sm, fb, c0, consts, *, batch, seq):
    blk = min(256, seq)
    sel, ones = consts
    body = functools.partial(_cumgate_body, seq=seq, blk=blk)
    c2 = lambda a: pl.BlockSpec(a.shape, lambda b: (0, 0))
    return pl.pallas_call(
        body,
        grid=(batch,),
        in_specs=[pl.BlockSpec((seq, LANES), lambda b: (b, 0)), c2(fb), c2(c0), c2(sel), c2(ones)],
        out_specs=[pl.BlockSpec((seq, LANES), lambda b: (b, 0)), pl.BlockSpec((seq, FOX_HD), lambda b: (b, 0)),
                   pl.BlockSpec((seq, FOX_HD), lambda b: (b, 0))],
        out_shape=[jax.ShapeDtypeStruct((batch * seq, LANES), F32),
                   jax.ShapeDtypeStruct((batch * seq, FOX_HD), BF16),
                   jax.ShapeDtypeStruct((batch * seq, FOX_HD), BF16)],
        compiler_params=pltpu.CompilerParams(dimension_semantics=("arbitrary",), vmem_limit_bytes=VMEM_LIMIT),
        name="cumgate",
    )(sm, fb, c0, sel, ones)


def _fox_body(q_ref, qb_ref, k_ref, kb_ref, v_ref, km_ref, kbm_ref, vm_ref, o_ref,
              s_scr, p_scr, m_scr, alpha_scr, acc_scr, *, tq):
    i = pl.program_id(2)
    wide = tq
    lane = lax.broadcasted_iota(jnp.int32, (1, LANES), 1)
    q_all = q_ref[...]
    qb_all = qb_ref[...]
    zero = jnp.zeros_like(q_all)
    one = jnp.ones_like(q_all[0:1, :])
    own = [lane < FOX_DH, lane >= FOX_DH]
    q_aug = []
    for hh in range(2):
        bias_lanes = (lane >= BIAS_STRIDE * hh) & (lane < BIAS_STRIDE * (hh + 1))
        q_aug.append(jnp.concatenate([jnp.where(own[hh], q_all, zero), jnp.where(bias_lanes, qb_all, zero)], axis=1))

    def keys(start, width):
        return jnp.concatenate([k_ref[pl.ds(start, width), :], kb_ref[pl.ds(start, width), :]], axis=1)

    def values(start, width, hh):
        return jnp.where(own[hh], v_ref[pl.ds(start, width), :], one)

    n_full = i
    two = range(2)

    diag_start = pl.multiple_of(n_full * wide, wide)
    meta_mask = lax.broadcasted_iota(jnp.int32, (tq, LANES), 1) < N_META
    causal = lax.broadcasted_iota(jnp.int32, (tq, wide), 1) <= lax.broadcasted_iota(jnp.int32, (tq, wide), 0)
    km_aug = jnp.concatenate([km_ref[...], kbm_ref[...]], axis=1)
    kd_aug = keys(diag_start, wide)
    s_meta = [jnp.where(meta_mask, _nt_dot(q_aug[hh], km_aug), MASK_VALUE) for hh in two]
    s_diag = [jnp.where(causal, _nt_dot(q_aug[hh], kd_aug), MASK_VALUE) for hh in two]

    n_pairs = (n_full + 1) // 2

    def block_start(t):
        return pl.multiple_of(jnp.clip(t, 0, jnp.maximum(n_full - 1, 0)) * wide, wide)

    def pipeline_step(u, carry):
        one_step(2 * u, 0, None)
        one_step(2 * u + 1, 1, 2 * u + 1 < n_full)
        return carry

    def one_step(t, cur, real_block):
        nxt = 1 - cur
        prev_start = block_start(t - 1)
        next_aug = keys(block_start(t + 1), wide)
        for hh in two:
            acc_scr[hh] = alpha_scr[hh] * acc_scr[hh] + _dot(p_scr[nxt, hh], values(prev_start, wide, hh))
        for hh in two:
            m_old = m_scr[hh]
            m_new = jnp.maximum(m_old, jnp.max(s_scr[cur, hh], axis=-1, keepdims=True))
            if real_block is not None:
                m_new = jnp.where(real_block, m_new, m_old)
            alpha_scr[hh] = jnp.exp2(m_old - m_new)
            m_scr[hh] = m_new
            p_scr[cur, hh] = jnp.exp2(s_scr[cur, hh] - m_new).astype(BF16)
        for hh in two:
            s_scr[nxt, hh] = _nt_dot(q_aug[hh], next_aug)

    first_aug = keys(block_start(0), wide)
    for hh in two:
        s_scr[0, hh] = _nt_dot(q_aug[hh], first_aug)
        p_scr[1, hh] = jnp.zeros((tq, wide), BF16)
        alpha_scr[hh] = jnp.ones((tq, 1), F32)
        m_scr[hh] = jnp.full((tq, 1), MASK_VALUE, F32)
        acc_scr[hh] = jnp.zeros((tq, LANES), F32)
    lax.fori_loop(0, n_pairs, pipeline_step, 0)
    t_last = 2 * n_pairs - 1
    last_start = block_start(t_last)
    keep_last = jnp.where(t_last < n_full, 1.0, 0.0)

    m_a = [jnp.maximum(jnp.max(s_meta[hh], axis=-1, keepdims=True), jnp.max(s_diag[hh], axis=-1, keepdims=True))
           for hh in two]
    acc_a = [_dot(jnp.exp2(s_meta[hh] - m_a[hh]).astype(BF16), jnp.where(own[hh], vm_ref[...], one))
             + _dot(jnp.exp2(s_diag[hh] - m_a[hh]).astype(BF16), values(diag_start, wide, hh)) for hh in two]

    outs = []
    for hh in two:
        acc_b = alpha_scr[hh] * acc_scr[hh] + keep_last * _dot(p_scr[1, hh], values(last_start, wide, hh))
        m_b = m_scr[hh]
        m = jnp.maximum(m_a[hh], m_b)
        acc = jnp.exp2(m_a[hh] - m) * acc_a[hh] + jnp.exp2(m_b - m) * acc_b
        outs.append(acc / pltpu.roll(acc, FOX_DH, 1))
    o_ref[...] = jnp.where(own[0], outs[0], outs[1]).astype(o_ref.dtype)


def _fox(fq, qb, fk, kb, fv, km, kbm, vm, *, batch, seq, tq):
    nq = seq // tq
    npair = FOX_HEADS // 2
    body = functools.partial(_fox_body, tq=tq)
    qspec = pl.BlockSpec((tq, LANES), lambda b, p, i: (b * nq + i, p))
    kspec = pl.BlockSpec((seq, LANES), lambda b, p, i: (b, p))
    mspec = pl.BlockSpec((LANES, LANES), lambda b, p, i: (0, p))
    return pl.pallas_call(
        body,
        grid=(batch, npair, nq),
        in_specs=[qspec, qspec, kspec, kspec, kspec, mspec, mspec, mspec],
        out_specs=qspec,
        out_shape=jax.ShapeDtypeStruct((batch * seq, FOX_HD), BF16),
        scratch_shapes=[pltpu.VMEM((2, 2, tq, tq), F32), pltpu.VMEM((2, 2, tq, tq), BF16),
                        pltpu.VMEM((2, tq, 1), F32), pltpu.VMEM((2, tq, 1), F32), pltpu.VMEM((2, tq, LANES), F32)],
        compiler_params=pltpu.CompilerParams(dimension_semantics=("arbitrary", "arbitrary", "arbitrary"),
                                             vmem_limit_bytes=VMEM_LIMIT),
        name="fox",
    )(fq, qb, fk, kb, fv, km, kbm, vm)


def _ffn_body(x_ref, og_ref, of_ref, wo_ref, nw_ref, wg_ref, wu_ref, wd_ref, fw_ref, out_ref, act_ref, *, fchunk):
    h1 = x_ref[...] + _dot(og_ref[...], wo_ref[0:GDN_HD, :]) + _dot(of_ref[...], wo_ref[GDN_HD:GDN_HD + FOX_HD, :])
    n = (h1 * lax.rsqrt(jnp.mean(h1 * h1, axis=-1, keepdims=True) + EPS) * nw_ref[...]).astype(BF16)
    d_ff = wg_ref.shape[1]
    for c in range(d_ff // fchunk):
        sl = slice(c * fchunk, (c + 1) * fchunk)
        g = _dot(n, wg_ref[:, sl])
        up = _dot(n, wu_ref[:, sl])
        act_ref[:, sl] = (_silu(g) * up).astype(BF16)
    acc = h1 + _dot(act_ref[...], wd_ref[...])
    out_ref[...] = acc * lax.rsqrt(jnp.mean(acc * acc, axis=-1, keepdims=True) + EPS) * fw_ref[...]


def _ffn(x2d, og, of, wo, nw, wg, wu, wd, fw, tm):
    rows = x2d.shape[0]
    row_spec = lambda n: pl.BlockSpec((tm, n), lambda i: (i, 0))
    const_spec = lambda a: pl.BlockSpec(a.shape, lambda i: (0, 0), pipeline_mode=pl.Buffered(1))
    body = functools.partial(_ffn_body, fchunk=256)
    return pl.pallas_call(
        body,
        grid=(rows // tm,),
        in_specs=[row_spec(D_MODEL), row_spec(GDN_HD), row_spec(FOX_HD), const_spec(wo), const_spec(nw),
                  const_spec(wg), const_spec(wu), const_spec(wd), const_spec(fw)],
        out_specs=row_spec(D_MODEL),
        out_shape=jax.ShapeDtypeStruct((rows, D_MODEL), F32),
        scratch_shapes=[pltpu.VMEM((tm, wg.shape[1]), BF16)],
        compiler_params=pltpu.CompilerParams(dimension_semantics=("arbitrary",), vmem_limit_bytes=VMEM_LIMIT),
        name="ffn",
    )(x2d, og, of, wo, nw, wg, wu, wd, fw)


def _lane_vec(vals, offset):
    return jnp.zeros((1, LANES), F32).at[0, offset:offset + vals.shape[0]].set(vals.astype(F32))


def kernel(x, meta_tokens, attn_norm_w, w_in, conv_w, a_log, dt_bias, gdn_norm_w, fgate_b, w_out, ffn_norm_w,
           w_gate, w_up, w_down, final_norm_w):
    batch, seq, _ = x.shape
    depth = w_in.shape[0]
    assert depth == 1 and seq % 512 == 0
    w = w_in[0]
    s = [0, 512, 1024, 1536, 2048, 2052, 2056, 2568, 3080, 3592, 3600]
    wbig = jnp.concatenate([w[:, s[0]:s[4]], w[:, s[6]:s[9]]], axis=1).astype(BF16)
    wsm = jnp.concatenate([w[:, s[4]:s[6]], w[:, s[9]:s[10]],
                           jnp.zeros((D_MODEL, LANES - 2 * GDN_HEADS - FOX_HEADS), F32)], axis=1).astype(BF16)
    anw = attn_norm_w[0][None, :]
    avec = _lane_vec(-jnp.exp(a_log[0]), LANE_DECAY)
    dtvec = _lane_vec(dt_bias[0], LANE_DECAY)
    fbvec = _lane_vec(fgate_b[0], LANE_FORGET)
    gnw = gdn_norm_w[0][None, :]
    convw = conv_w[0]
    bias_consts = _bias_constants()

    x2d = x.reshape(batch * seq, D_MODEL)

    qkv_m, z_m, _, fk_m, fv_m, sm_m = _inproj(meta_tokens, anw, wbig, wsm, tm=N_META)
    zeros_hist = jnp.zeros((SUBLANES, GDN_QKV), F32)
    zeros_state = jnp.zeros((GDN_HEADS, GDN_D, GDN_D), F32)
    _, s_meta = _gdn(qkv_m, z_m, sm_m, convw, zeros_hist, zeros_state, avec, dtvec, gnw,
                     batch=1, seq=N_META, rows=N_META)
    ccol_m, _, kb_m = _cumgate(sm_m, fbvec, jnp.zeros((1, LANES), F32), bias_consts, batch=1, seq=N_META)
    pad_rows = ((0, LANES - N_META), (0, 0))
    km = jnp.pad(fk_m, pad_rows)
    kbm = jnp.pad(kb_m, pad_rows)
    vm = jnp.pad(fv_m, pad_rows)

    qkv, z, fq, fk, fv, sm = _inproj(x2d, anw, wbig, wsm, tm=512)
    o_gdn, _ = _gdn(qkv, z, sm, convw, qkv_m[N_META - SUBLANES:], s_meta[0], avec, dtvec, gnw,
                    batch=batch, seq=seq, rows=256)
    _, qb, kb = _cumgate(sm, fbvec, ccol_m[N_META - 1:], bias_consts, batch=batch, seq=seq)
    o_fox = _fox(fq, qb, fk, kb, fv, km, kbm, vm, batch=batch, seq=seq, tq=512)

    out = _ffn(x2d, o_gdn, o_fox, w_out[0].astype(BF16), ffn_norm_w[0][None, :], w_gate[0].astype(BF16),
               w_up[0].astype(BF16), w_down[0].astype(BF16), final_norm_w[None, :], tm=512)
    return out.reshape(batch, seq, D_MODEL)
```

```python
import functools
import math

import numpy as np
import jax
import jax.numpy as jnp
from jax import lax
from jax.experimental import pallas as pl
from jax.experimental.pallas import tpu as pltpu

F32 = jnp.float32
BF16 = jnp.bfloat16

D_MODEL = 1024
N_META = 16
GDN_HEADS = 4
GDN_D = 128
FOX_HEADS = 8
FOX_DH = 64
CONV_K = 4
EPS = 1e-6
MASK_VALUE = -1e30
LOG2E = 1.4426950408889634
GDN_QKV = 3 * GDN_HEADS * GDN_D
GDN_HD = GDN_HEADS * GDN_D
FOX_HD = FOX_HEADS * FOX_DH
LANES = 128
SUBLANES = 8
LANE_BETA = 0
LANE_DECAY = GDN_HEADS
LANE_FORGET = 2 * GDN_HEADS
BIAS_STRIDE = 8
INV_BASE = 16
VMEM_LIMIT = 56 * 1024 * 1024


def _nt_dot(a, b):
    return lax.dot_general(a, b, (((1,), (1,)), ((), ())), preferred_element_type=F32)


def _tn_dot(a, b):
    return lax.dot_general(a, b, (((0,), (0,)), ((), ())), preferred_element_type=F32)


def _dot(a, b):
    return jnp.dot(a, b, preferred_element_type=F32)


def _split3(x):
    x1 = x.astype(BF16)
    r1 = x - x1.astype(F32)
    x2 = r1.astype(BF16)
    x3 = (r1 - x2.astype(F32)).astype(BF16)
    return x1, x2, x3


def _zero_after(x):
    bits = lax.bitcast_convert_type(x[0:SUBLANES, 0:LANES], jnp.uint32)
    zero = lax.shift_right_logical(lax.shift_right_logical(bits, jnp.uint32(16)), jnp.uint32(16))
    return lax.bitcast_convert_type(zero, F32)[0:1, :]


def _softplus(x):
    return jnp.maximum(x, 0.0) + jnp.log1p(jnp.exp(-jnp.abs(x)))


def _sigmoid(x):
    return 1.0 / (1.0 + jnp.exp(-x))


def _silu(x):
    return x * _sigmoid(x)


def _inproj_body(x_ref, nw_ref, wbig_ref, wsm_ref, qkv_ref, z_ref, fq_ref, fk_ref, fv_ref, sm_ref):
    x = x_ref[...]
    u = (x * lax.rsqrt(jnp.mean(x * x, axis=-1, keepdims=True) + EPS) * nw_ref[...]).astype(BF16)
    o0 = GDN_QKV
    o1 = o0 + GDN_HD
    o2 = o1 + FOX_HD
    o3 = o2 + FOX_HD
    o4 = o3 + FOX_HD
    qkv_ref[...] = _dot(u, wbig_ref[:, 0:o0])
    z_ref[...] = _dot(u, wbig_ref[:, o0:o1])
    fq_ref[...] = (_dot(u, wbig_ref[:, o1:o2]) * (LOG2E * FOX_DH ** -0.5)).astype(BF16)
    fk_ref[...] = _dot(u, wbig_ref[:, o2:o3]).astype(BF16)
    fv_ref[...] = _dot(u, wbig_ref[:, o3:o4]).astype(BF16)
    sm_ref[...] = _dot(u, wsm_ref[...])


def _inproj(x2d, nw, wbig, wsm, tm):
    rows = x2d.shape[0]
    row_spec = lambda n: pl.BlockSpec((tm, n), lambda i: (i, 0))
    const_spec = lambda a: pl.BlockSpec(a.shape, lambda i: (0, 0))
    return pl.pallas_call(
        _inproj_body,
        grid=(rows // tm,),
        in_specs=[row_spec(D_MODEL), const_spec(nw), const_spec(wbig), const_spec(wsm)],
        out_specs=[row_spec(GDN_QKV), row_spec(GDN_HD), row_spec(FOX_HD), row_spec(FOX_HD), row_spec(FOX_HD),
                   row_spec(LANES)],
        out_shape=[jax.ShapeDtypeStruct((rows, GDN_QKV), F32), jax.ShapeDtypeStruct((rows, GDN_HD), F32),
                   jax.ShapeDtypeStruct((rows, FOX_HD), BF16), jax.ShapeDtypeStruct((rows, FOX_HD), BF16),
                   jax.ShapeDtypeStruct((rows, FOX_HD), BF16), jax.ShapeDtypeStruct((rows, LANES), F32)],
        compiler_params=pltpu.CompilerParams(dimension_semantics=("arbitrary",), vmem_limit_bytes=VMEM_LIMIT),
        name="inproj",
    )(x2d, nw, wbig, wsm)


def _unit_lower_inverses(a_mats, r_i, c_i, size, fillers):
    fillers = iter(fillers)

    def fill(anchor):
        thunk = next(fillers, None)
        if thunk is not None:
            thunk(_zero_after(anchor))

    n = range(len(a_mats))
    base = min(INV_BASE, size)
    eye = jnp.where(r_i == c_i, 1.0, 0.0).astype(F32)
    rc_xor = r_i ^ c_i
    p_pow = [jnp.where(rc_xor < base, -a_mats[i], 0.0) for i in n]
    t_inv = [eye + p_pow[i] for i in n]
    for _ in range(int(math.log2(base)) - 1):
        p16 = [p_pow[i].astype(BF16) for i in n]
        p_pow = [_dot(p16[i], p16[i]) for i in n]
        fill(p_pow[0])
        t_upd = [_dot(t_inv[i].astype(BF16), p_pow[i].astype(BF16)) for i in n]
        fill(t_upd[0])
        t_inv = [t_inv[i] + t_upd[i] for i in n]
    half = base
    while half < size:
        lower_left = (rc_xor >= half) & (rc_xor < 2 * half)
        t16 = [t_inv[i].astype(BF16) for i in n]
        y = [_dot(jnp.where(lower_left, a_mats[i], 0.0).astype(BF16), t16[i]) for i in n]
        fill(y[0])
        t_upd = [_dot(t16[i], y[i].astype(BF16)) for i in n]
        fill(t_upd[0])
        t_inv = [t_inv[i] - t_upd[i] for i in n]
        half *= 2
    for thunk in fillers:
        thunk(_zero_after(t_inv[0]))
    return t_inv


def _gdn_body(qkv_ref, qkv_next_ref, z_ref, sm_ref, convw_ref, hist_ref, s0_ref, avec_ref, dtvec_ref, normw_ref,
              o_ref, sout_ref, state_ref, xpad_ref, act_ref, *, rows, lookahead):
    j = pl.program_id(1)
    cur = j % 2

    def conv_silu(lo, hi, zero_row=None):
        first = SUBLANES - (CONV_K - 1)
        w_first = convw_ref[0:1, lo:hi] if zero_row is None else convw_ref[0:1, lo:hi] + zero_row
        conv = w_first * xpad_ref[first:first + rows, lo:hi]
        for kk in range(1, CONV_K):
            conv = conv + convw_ref[kk:kk + 1, lo:hi] * xpad_ref[first + kk:first + kk + rows, lo:hi]
        return _silu(conv)

    def keep_history():
        xpad_ref[0:SUBLANES, :] = xpad_ref[rows:rows + SUBLANES, :]

    @pl.when(j == 0)
    def _():
        state_ref[...] = s0_ref[...]
        xpad_ref[0:SUBLANES, :] = hist_ref[...]
        xpad_ref[SUBLANES:SUBLANES + rows, :] = qkv_ref[...]
        act_ref[0] = conv_silu(0, GDN_QKV)
        keep_history()

    fillers = []
    if lookahead:
        def stage_next_block(zero_row):
            xpad_ref[SUBLANES:SUBLANES + rows, :] = qkv_next_ref[...]
        fillers.append(stage_next_block)
        for lo in range(0, GDN_QKV, GDN_D):
            def conv_slice(zero_row, lo=lo):
                act_ref[1 - cur, :, lo:lo + GDN_D] = conv_silu(lo, lo + GDN_D, zero_row)
            fillers.append(conv_slice)
        fillers.append(lambda zero_row: keep_history())

    sm = sm_ref[...]
    beta_all = _sigmoid(sm)
    g_all = avec_ref[...] * _softplus(sm + dtvec_ref[...])

    r_i = lax.broadcasted_iota(jnp.int32, (rows, rows), 0)
    c_i = lax.broadcasted_iota(jnp.int32, (rows, rows), 1)
    incl = r_i >= c_i
    strict = r_i > c_i
    tri = jnp.where(incl, 1.0, 0.0).astype(BF16)
    g1, g2, g3 = _split3(g_all)
    gc_all = _dot(tri, g1) + _dot(tri, g2) + _dot(tri, g3)
    sel = jnp.where(lax.broadcasted_iota(jnp.int32, (SUBLANES, LANES), 1)
                    == lax.broadcasted_iota(jnp.int32, (SUBLANES, LANES), 0) + LANE_DECAY, 1.0, 0.0).astype(BF16)
    c1, c2, c3 = _split3(gc_all)
    gc_rows = _nt_dot(sel, c1) + _nt_dot(sel, c2) + _nt_dot(sel, c3)
    egc_all = jnp.exp(gc_all)
    gl_row = gc_all[rows - 1:rows, :]
    ekg_all = jnp.exp(gl_row - gc_all)
    egl_all = jnp.exp(gl_row)
    normw = normw_ref[...]

    heads = range(GDN_HEADS)
    qs, ks, kbs, k16s, rhss, decays, egcs, ekgs, egls = [], [], [], [], [], [], [], [], []
    for h in heads:
        q = act_ref[cur, :, h * GDN_D:(h + 1) * GDN_D]
        k = act_ref[cur, :, GDN_HD + h * GDN_D:GDN_HD + (h + 1) * GDN_D]
        v = act_ref[cur, :, 2 * GDN_HD + h * GDN_D:2 * GDN_HD + (h + 1) * GDN_D]
        q = q * lax.rsqrt(jnp.sum(q * q, axis=-1, keepdims=True) + EPS) * (GDN_D ** -0.5)
        k = k * lax.rsqrt(jnp.sum(k * k, axis=-1, keepdims=True) + EPS)
        beta = beta_all[:, LANE_BETA + h:LANE_BETA + h + 1]
        gcol = gc_all[:, LANE_DECAY + h:LANE_DECAY + h + 1]
        egc = egc_all[:, LANE_DECAY + h:LANE_DECAY + h + 1]
        grow = gc_rows[h:h + 1, :]
        kb = k * beta
        qs.append(q)
        ks.append(k)
        kbs.append(kb)
        k16s.append(k.astype(BF16))
        egcs.append(egc)
        ekgs.append(ekg_all[:, LANE_DECAY + h:LANE_DECAY + h + 1])
        egls.append(egl_all[:, LANE_DECAY + h:LANE_DECAY + h + 1])
        decays.append(jnp.where(incl, jnp.exp(jnp.where(incl, gcol - grow, 0.0)), 0.0))
        rhss.append(jnp.concatenate([v * beta, kb * egc], axis=1).astype(BF16))
    a_mats = [jnp.where(strict, _nt_dot(kbs[h].astype(BF16), k16s[h]) * decays[h], 0.0) for h in heads]
    qks = [jnp.where(incl, _nt_dot(qs[h].astype(BF16), k16s[h]) * decays[h], 0.0).astype(BF16) for h in heads]
    t_invs = _unit_lower_inverses(a_mats, r_i, c_i, rows, fillers)
    uws = [_dot(t_invs[h].astype(BF16), rhss[h]) for h in heads]
    s_old = [state_ref[h] for h in heads]
    wss = [_dot(jnp.concatenate([uws[h][:, GDN_D:], qs[h] * egcs[h]], axis=0).astype(BF16), s_old[h].astype(BF16))
           for h in heads]
    v_news = [(uws[h][:, :GDN_D] - wss[h][:rows]).astype(BF16) for h in heads]
    outs = [wss[h][rows:] + _dot(qks[h], v_news[h]) for h in heads]
    for h in heads:
        state_ref[h] = s_old[h] * egls[h] + _tn_dot((ks[h] * ekgs[h]).astype(BF16), v_news[h])
    for h in heads:
        zg = _silu(z_ref[:, h * GDN_D:(h + 1) * GDN_D])
        o = outs[h]
        o = o * lax.rsqrt(jnp.mean(o * o, axis=-1, keepdims=True) + EPS) * normw * zg
        o_ref[:, h * GDN_D:(h + 1) * GDN_D] = o.astype(o_ref.dtype)

    @pl.when(j == pl.num_programs(1) - 1)
    def _():
        sout_ref[0] = state_ref[...]


def _gdn(qkv, z, sm, convw, hist, s0, avec, dtvec, normw, *, batch, seq, rows):
    nblk = seq // rows
    row_spec = lambda n: pl.BlockSpec((rows, n), lambda b, j: (b * nblk + j, 0))
    next_spec = pl.BlockSpec((rows, GDN_QKV), lambda b, j: (b * nblk + jnp.minimum(j + 1, nblk - 1), 0))
    const2 = lambda a: pl.BlockSpec(a.shape, lambda b, j: (0, 0))
    body = functools.partial(_gdn_body, rows=rows, lookahead=nblk > 1)
    return pl.pallas_call(
        body,
        grid=(batch, nblk),
        in_specs=[row_spec(GDN_QKV), next_spec, row_spec(GDN_HD), row_spec(LANES), const2(convw), const2(hist),
                  pl.BlockSpec(s0.shape, lambda b, j: (0, 0, 0)), const2(avec), const2(dtvec), const2(normw)],
        out_specs=[row_spec(GDN_HD), pl.BlockSpec((1, GDN_HEADS, GDN_D, GDN_D), lambda b, j: (b, 0, 0, 0))],
        out_shape=[jax.ShapeDtypeStruct((batch * seq, GDN_HD), BF16),
                   jax.ShapeDtypeStruct((batch, GDN_HEADS, GDN_D, GDN_D), F32)],
        scratch_shapes=[pltpu.VMEM((GDN_HEADS, GDN_D, GDN_D), F32),
                        pltpu.VMEM((rows + SUBLANES, GDN_QKV), F32),
                        pltpu.VMEM((2, rows, GDN_QKV), F32)],
        compiler_params=pltpu.CompilerParams(dimension_semantics=("arbitrary", "arbitrary"),
                                             vmem_limit_bytes=VMEM_LIMIT),
        name="gdn",
    )(qkv, qkv, z, sm, convw, hist, s0, avec, dtvec, normw)


def _bias_constants():
    sel = np.zeros((LANES, 2 * FOX_HD), np.float32)
    ones = np.zeros((1, 2 * FOX_HD), np.float32)
    for head in range(FOX_HEADS):
        base = (head // 2) * LANES + BIAS_STRIDE * (head % 2)
        for t in range(3):
            src = LANE_FORGET + t * FOX_HEADS + head
            sel[src, base + t] = 1.0
            sel[src, FOX_HD + base + 3 + t] = -1.0
            ones[0, base + 3 + t] = 1.0
            ones[0, FOX_HD + base + t] = 1.0
    return jnp.asarray(sel, BF16), jnp.asarray(ones)


def _pack3(x):
    hi = x.astype(BF16).astype(F32)
    rest = x - hi
    mid = rest.astype(BF16).astype(F32)
    lo = (rest - mid).astype(BF16).astype(F32)
    return (hi + pltpu.roll(mid, FOX_HEADS, 1) + pltpu.roll(lo, 2 * FOX_HEADS, 1)).astype(BF16)


def _cumgate_body(sm_ref, fb_ref, c0_ref, sel_ref, ones_ref, ccol_ref, qb_ref, kb_ref, *, seq, blk):
    lane = lax.broadcasted_iota(jnp.int32, (1, LANES), 1)
    valid = (lane >= LANE_FORGET) & (lane < LANE_FORGET + FOX_HEADS)
    r_i = lax.broadcasted_iota(jnp.int32, (blk, blk), 0)
    c_i = lax.broadcasted_iota(jnp.int32, (blk, blk), 1)
    tri = jnp.where(r_i >= c_i, 1.0, 0.0).astype(BF16)
    carry = c0_ref[...]
    for i in range(seq // blk):
        xg = sm_ref[i * blk:(i + 1) * blk, :] + fb_ref[...]
        logf = jnp.where(valid, jnp.minimum(xg, 0.0) - jnp.log1p(jnp.exp(-jnp.abs(xg))), 0.0)
        cs3 = _dot(tri, _pack3(logf))
        cs = cs3 + pltpu.roll(cs3, LANES - FOX_HEADS, 1) + pltpu.roll(cs3, LANES - 2 * FOX_HEADS, 1)
        cs = jnp.where(valid, carry + cs, 0.0)
        ccol_ref[i * blk:(i + 1) * blk, :] = cs
        bias = _dot(_pack3(cs * LOG2E), sel_ref[...]) + ones_ref[...]
        qb_ref[i * blk:(i + 1) * blk, :] = bias[:, :FOX_HD].astype(BF16)
        kb_ref[i * blk:(i + 1) * blk, :] = bias[:, FOX_HD:].astype(BF16)
        carry = cs[blk - 1:blk, :]


def _cumgate(sm, fb, c0, consts, *, batch, seq):
    blk = min(256, seq)
    sel, ones = consts
    body = functools.partial(_cumgate_body, seq=seq, blk=blk)
    c2 = lambda a: pl.BlockSpec(a.shape, lambda b: (0, 0))
    return pl.pallas_call(
        body,
        grid=(batch,),
        in_specs=[pl.BlockSpec((seq, LANES), lambda b: (b, 0)), c2(fb), c2(c0), c2(sel), c2(ones)],
        out_specs=[pl.BlockSpec((seq, LANES), lambda b: (b, 0)), pl.BlockSpec((seq, FOX_HD), lambda b: (b, 0)),
                   pl.BlockSpec((seq, FOX_HD), lambda b: (b, 0))],
        out_shape=[jax.ShapeDtypeStruct((batch * seq, LANES), F32),
                   jax.ShapeDtypeStruct((batch * seq, FOX_HD), BF16),
                   jax.ShapeDtypeStruct((batch * seq, FOX_HD), BF16)],
        compiler_params=pltpu.CompilerParams(dimension_semantics=("arbitrary",), vmem_limit_bytes=VMEM_LIMIT),
        name="cumgate",
    )(sm, fb, c0, sel, ones)


def _fox_body(q_ref, qb_ref, k_ref, kb_ref, v_ref, km_ref, kbm_ref, vm_ref, o_ref,
              s_scr, p_scr, m_scr, alpha_scr, acc_scr, *, tq):
    i = pl.program_id(2)
    wide = tq
    lane = lax.broadcasted_iota(jnp.int32, (1, LANES), 1)
    q_all = q_ref[...]
    qb_all = qb_ref[...]
    zero = jnp.zeros_like(q_all)
    one = jnp.ones_like(q_all[0:1, :])
    own = [lane < FOX_DH, lane >= FOX_DH]
    q_aug = []
    for hh in range(2):
        bias_lanes = (lane >= BIAS_STRIDE * hh) & (lane < BIAS_STRIDE * (hh + 1))
        q_aug.append(jnp.concatenate([jnp.where(own[hh], q_all, zero), jnp.where(bias_lanes, qb_all, zero)], axis=1))

    def keys(start, width):
        return jnp.concatenate([k_ref[pl.ds(start, width), :], kb_ref[pl.ds(start, width), :]], axis=1)

    def values(start, width, hh):
        return jnp.where(own[hh], v_ref[pl.ds(start, width), :], one)

    n_full = i
    two = range(2)

    diag_start = pl.multiple_of(n_full * wide, wide)
    meta_mask = lax.broadcasted_iota(jnp.int32, (tq, LANES), 1) < N_META
    causal = lax.broadcasted_iota(jnp.int32, (tq, wide), 1) <= lax.broadcasted_iota(jnp.int32, (tq, wide), 0)
    km_aug = jnp.concatenate([km_ref[...], kbm_ref[...]], axis=1)
    kd_aug = keys(diag_start, wide)
    s_meta = [jnp.where(meta_mask, _nt_dot(q_aug[hh], km_aug), MASK_VALUE) for hh in two]
    s_diag = [jnp.where(causal, _nt_dot(q_aug[hh], kd_aug), MASK_VALUE) for hh in two]

    n_pairs = (n_full + 1) // 2

    def block_start(t):
        return pl.multiple_of(jnp.clip(t, 0, jnp.maximum(n_full - 1, 0)) * wide, wide)

    def value_start(t):
        return pl.multiple_of(jnp.where(t < 0, diag_start, block_start(t)), wide)

    def pipeline_step(u, carry):
        one_step(2 * u, 0, None)
        one_step(2 * u + 1, 1, 2 * u + 1 < n_full)
        return carry

    def one_step(t, cur, real_block):
        nxt = 1 - cur
        prev_start = value_start(t - 1)
        next_aug = keys(block_start(t + 1), wide)
        for hh in two:
            acc_scr[hh] = alpha_scr[hh] * acc_scr[hh] + _dot(p_scr[nxt, hh], values(prev_start, wide, hh))
        for hh in two:
            m_old = m_scr[hh]
            m_new = jnp.maximum(m_old, jnp.max(s_scr[cur, hh], axis=-1, keepdims=True))
            if real_block is not None:
                m_new = jnp.where(real_block, m_new, m_old)
            alpha_scr[hh] = jnp.exp2(m_old - m_new)
            m_scr[hh] = m_new
            p_scr[cur, hh] = jnp.exp2(s_scr[cur, hh] - m_new).astype(BF16)
        for hh in two:
            s_scr[nxt, hh] = _nt_dot(q_aug[hh], next_aug)

    first_aug = keys(block_start(0), wide)
    for hh in two:
        s_scr[0, hh] = _nt_dot(q_aug[hh], first_aug)
    for hh in two:
        m_first = jnp.maximum(jnp.max(s_meta[hh], axis=-1, keepdims=True),
                              jnp.max(s_diag[hh], axis=-1, keepdims=True))
        m_scr[hh] = m_first
        alpha_scr[hh] = jnp.ones((tq, 1), F32)
        p_scr[1, hh] = jnp.exp2(s_diag[hh] - m_first).astype(BF16)
        acc_scr[hh] = _dot(jnp.exp2(s_meta[hh] - m_first).astype(BF16), jnp.where(own[hh], vm_ref[...], one))
    lax.fori_loop(0, n_pairs, pipeline_step, 0)

    t_last = 2 * n_pairs - 1
    last_start = value_start(t_last)
    keep_last = jnp.where(t_last < n_full, 1.0, 0.0)
    outs = []
    for hh in two:
        acc = alpha_scr[hh] * acc_scr[hh] + keep_last * _dot(p_scr[1, hh], values(last_start, wide, hh))
        outs.append(acc / pltpu.roll(acc, FOX_DH, 1))
    o_ref[...] = jnp.where(own[0], outs[0], outs[1]).astype(o_ref.dtype)


def _fox(fq, qb, fk, kb, fv, km, kbm, vm, *, batch, seq, tq):
    nq = seq // tq
    npair = FOX_HEADS // 2
    body = functools.partial(_fox_body, tq=tq)
    qspec = pl.BlockSpec((tq, LANES), lambda b, p, i: (b * nq + i, p))
    kspec = pl.BlockSpec((seq, LANES), lambda b, p, i: (b, p))
    mspec = pl.BlockSpec((LANES, LANES), lambda b, p, i: (0, p))
    return pl.pallas_call(
        body,
        grid=(batch, npair, nq),
        in_specs=[qspec, qspec, kspec, kspec, kspec, mspec, mspec, mspec],
        out_specs=qspec,
        out_shape=jax.ShapeDtypeStruct((batch * seq, FOX_HD), BF16),
        scratch_shapes=[pltpu.VMEM((2, 2, tq, tq), F32), pltpu.VMEM((2, 2, tq, tq), BF16),
                        pltpu.VMEM((2, tq, 1), F32), pltpu.VMEM((2, tq, 1), F32), pltpu.VMEM((2, tq, LANES), F32)],
        compiler_params=pltpu.CompilerParams(dimension_semantics=("arbitrary", "arbitrary", "arbitrary"),
                                             vmem_limit_bytes=VMEM_LIMIT),
        name="fox",
    )(fq, qb, fk, kb, fv, km, kbm, vm)


def _ffn_body(x_ref, og_ref, of_ref, wo_ref, nw_ref, wg_ref, wu_ref, wd_ref, fw_ref, out_ref, act_ref, *, fchunk):
    h1 = x_ref[...] + _dot(og_ref[...], wo_ref[0:GDN_HD, :]) + _dot(of_ref[...], wo_ref[GDN_HD:GDN_HD + FOX_HD, :])
    n = (h1 * lax.rsqrt(jnp.mean(h1 * h1, axis=-1, keepdims=True) + EPS) * nw_ref[...]).astype(BF16)
    d_ff = wg_ref.shape[1]
    for c in range(d_ff // fchunk):
        sl = slice(c * fchunk, (c + 1) * fchunk)
        g = _dot(n, wg_ref[:, sl])
        up = _dot(n, wu_ref[:, sl])
        act_ref[:, sl] = (_silu(g) * up).astype(BF16)
    acc = h1 + _dot(act_ref[...], wd_ref[...])
    out_ref[...] = acc * lax.rsqrt(jnp.mean(acc * acc, axis=-1, keepdims=True) + EPS) * fw_ref[...]


def _ffn(x2d, og, of, wo, nw, wg, wu, wd, fw, tm):
    rows = x2d.shape[0]
    row_spec = lambda n: pl.BlockSpec((tm, n), lambda i: (i, 0))
    const_spec = lambda a: pl.BlockSpec(a.shape, lambda i: (0, 0), pipeline_mode=pl.Buffered(1))
    body = functools.partial(_ffn_body, fchunk=256)
    return pl.pallas_call(
        body,
        grid=(rows // tm,),
        in_specs=[row_spec(D_MODEL), row_spec(GDN_HD), row_spec(FOX_HD), const_spec(wo), const_spec(nw),
                  const_spec(wg), const_spec(wu), const_spec(wd), const_spec(fw)],
        out_specs=row_spec(D_MODEL),
        out_shape=jax.ShapeDtypeStruct((rows, D_MODEL), F32),
        scratch_shapes=[pltpu.VMEM((tm, wg.shape[1]), BF16)],
        compiler_params=pltpu.CompilerParams(dimension_semantics=("arbitrary",), vmem_limit_bytes=VMEM_LIMIT),
        name="ffn",
    )(x2d, og, of, wo, nw, wg, wu, wd, fw)


def _lane_vec(vals, offset):
    return jnp.zeros((1, LANES), F32).at[0, offset:offset + vals.shape[0]].set(vals.astype(F32))


def kernel(x, meta_tokens, attn_norm_w, w_in, conv_w, a_log, dt_bias, gdn_norm_w, fgate_b, w_out, ffn_norm_w,
           w_gate, w_up, w_down, final_norm_w):
    batch, seq, _ = x.shape
    depth = w_in.shape[0]
    assert depth == 1 and seq % 512 == 0
    w = w_in[0]
    s = [0, 512, 1024, 1536, 2048, 2052, 2056, 2568, 3080, 3592, 3600]
    wbig = jnp.concatenate([w[:, s[0]:s[4]], w[:, s[6]:s[9]]], axis=1).astype(BF16)
    wsm = jnp.concatenate([w[:, s[4]:s[6]], w[:, s[9]:s[10]],
                           jnp.zeros((D_MODEL, LANES - 2 * GDN_HEADS - FOX_HEADS), F32)], axis=1).astype(BF16)
    anw = attn_norm_w[0][None, :]
    avec = _lane_vec(-jnp.exp(a_log[0]), LANE_DECAY)
    dtvec = _lane_vec(dt_bias[0], LANE_DECAY)
    fbvec = _lane_vec(fgate_b[0], LANE_FORGET)
    gnw = gdn_norm_w[0][None, :]
    convw = conv_w[0]
    bias_consts = _bias_constants()

    x2d = x.reshape(batch * seq, D_MODEL)

    qkv_m, z_m, _, fk_m, fv_m, sm_m = _inproj(meta_tokens, anw, wbig, wsm, tm=N_META)
    zeros_hist = jnp.zeros((SUBLANES, GDN_QKV), F32)
    zeros_state = jnp.zeros((GDN_HEADS, GDN_D, GDN_D), F32)
    _, s_meta = _gdn(qkv_m, z_m, sm_m, convw, zeros_hist, zeros_state, avec, dtvec, gnw,
                     batch=1, seq=N_META, rows=N_META)
    ccol_m, _, kb_m = _cumgate(sm_m, fbvec, jnp.zeros((1, LANES), F32), bias_consts, batch=1, seq=N_META)
    pad_rows = ((0, LANES - N_META), (0, 0))
    km = jnp.pad(fk_m, pad_rows)
    kbm = jnp.pad(kb_m, pad_rows)
    vm = jnp.pad(fv_m, pad_rows)

    qkv, z, fq, fk, fv, sm = _inproj(x2d, anw, wbig, wsm, tm=512)
    o_gdn, _ = _gdn(qkv, z, sm, convw, qkv_m[N_META - SUBLANES:], s_meta[0], avec, dtvec, gnw,
                    batch=batch, seq=seq, rows=256)
    _, qb, kb = _cumgate(sm, fbvec, ccol_m[N_META - 1:], bias_consts, batch=batch, seq=seq)
    o_fox = _fox(fq, qb, fk, kb, fv, km, kbm, vm, batch=batch, seq=seq, tq=512)

    out = _ffn(x2d, o_gdn, o_fox, w_out[0].astype(BF16), ffn_norm_w[0][None, :], w_gate[0].astype(BF16),
               w_up[0].astype(BF16), w_down[0].astype(BF16), final_norm_w[None, :], tm=512)
    return out.reshape(batch, seq, D_MODEL)
```

```python
import functools
import math

import numpy as np
import jax
import jax.numpy as jnp
from jax import lax
from jax.experimental import pallas as pl
from jax.experimental.pallas import tpu as pltpu

F32 = jnp.float32
BF16 = jnp.bfloat16

D_MODEL = 1024
N_META = 16
GDN_HEADS = 4
GDN_D = 128
FOX_HEADS = 8
FOX_DH = 64
CONV_K = 4
EPS = 1e-6
MASK_VALUE = -1e30
LOG2E = 1.4426950408889634
GDN_QKV = 3 * GDN_HEADS * GDN_D
GDN_HD = GDN_HEADS * GDN_D
FOX_HD = FOX_HEADS * FOX_DH
LANES = 128
SUBLANES = 8
LANE_BETA = 0
LANE_DECAY = GDN_HEADS
LANE_FORGET = 2 * GDN_HEADS
BIAS_STRIDE = 8
INV_BASE = 16
VMEM_LIMIT = 56 * 1024 * 1024


def _nt_dot(a, b):
    return lax.dot_general(a, b, (((1,), (1,)), ((), ())), preferred_element_type=F32)


def _tn_dot(a, b):
    return lax.dot_general(a, b, (((0,), (0,)), ((), ())), preferred_element_type=F32)


def _dot(a, b):
    return jnp.dot(a, b, preferred_element_type=F32)


def _split3(x):
    x1 = x.astype(BF16)
    r1 = x - x1.astype(F32)
    x2 = r1.astype(BF16)
    x3 = (r1 - x2.astype(F32)).astype(BF16)
    return x1, x2, x3


def _zero_after(x):
    x = x[0:SUBLANES, 0:LANES]
    if x.shape[1] < LANES:
        x = jnp.concatenate([x] * (LANES // x.shape[1]), axis=1)
    bits = lax.bitcast_convert_type(x, jnp.uint32)
    zero = lax.shift_right_logical(lax.shift_right_logical(bits, jnp.uint32(16)), jnp.uint32(16))
    return lax.bitcast_convert_type(zero, F32)[0:1, :]


def _softplus(x):
    return jnp.maximum(x, 0.0) + jnp.log1p(jnp.exp(-jnp.abs(x)))


def _sigmoid(x):
    return 1.0 / (1.0 + jnp.exp(-x))


def _silu(x):
    return x * _sigmoid(x)


def _inproj_body(x_ref, nw_ref, wbig_ref, wsm_ref, qkv_ref, z_ref, fq_ref, fk_ref, fv_ref, sm_ref):
    x = x_ref[...]
    u = (x * lax.rsqrt(jnp.mean(x * x, axis=-1, keepdims=True) + EPS) * nw_ref[...]).astype(BF16)
    o0 = GDN_QKV
    o1 = o0 + GDN_HD
    o2 = o1 + FOX_HD
    o3 = o2 + FOX_HD
    o4 = o3 + FOX_HD
    qkv_ref[...] = _dot(u, wbig_ref[:, 0:o0])
    z_ref[...] = _dot(u, wbig_ref[:, o0:o1])
    fq_ref[...] = (_dot(u, wbig_ref[:, o1:o2]) * (LOG2E * FOX_DH ** -0.5)).astype(BF16)
    fk_ref[...] = _dot(u, wbig_ref[:, o2:o3]).astype(BF16)
    fv_ref[...] = _dot(u, wbig_ref[:, o3:o4]).astype(BF16)
    sm_ref[...] = _dot(u, wsm_ref[...])


def _inproj(x2d, nw, wbig, wsm, tm):
    rows = x2d.shape[0]
    row_spec = lambda n: pl.BlockSpec((tm, n), lambda i: (i, 0))
    const_spec = lambda a: pl.BlockSpec(a.shape, lambda i: (0, 0))
    return pl.pallas_call(
        _inproj_body,
        grid=(rows // tm,),
        in_specs=[row_spec(D_MODEL), const_spec(nw), const_spec(wbig), const_spec(wsm)],
        out_specs=[row_spec(GDN_QKV), row_spec(GDN_HD), row_spec(FOX_HD), row_spec(FOX_HD), row_spec(FOX_HD),
                   row_spec(LANES)],
        out_shape=[jax.ShapeDtypeStruct((rows, GDN_QKV), F32), jax.ShapeDtypeStruct((rows, GDN_HD), F32),
                   jax.ShapeDtypeStruct((rows, FOX_HD), BF16), jax.ShapeDtypeStruct((rows, FOX_HD), BF16),
                   jax.ShapeDtypeStruct((rows, FOX_HD), BF16), jax.ShapeDtypeStruct((rows, LANES), F32)],
        compiler_params=pltpu.CompilerParams(dimension_semantics=("arbitrary",), vmem_limit_bytes=VMEM_LIMIT),
        name="inproj",
    )(x2d, nw, wbig, wsm)


def _unit_lower_inverses(a_mats, r_i, c_i, size, fillers):
    fillers = iter(fillers)

    def fill(anchor):
        thunk = next(fillers, None)
        if thunk is not None:
            thunk(_zero_after(anchor))

    n = range(len(a_mats))
    base = min(INV_BASE, size)
    eye = jnp.where(r_i == c_i, 1.0, 0.0).astype(F32)
    rc_xor = r_i ^ c_i
    p_pow = [jnp.where(rc_xor < base, -a_mats[i], 0.0) for i in n]
    t_inv = [eye + p_pow[i] for i in n]
    for _ in range(int(math.log2(base)) - 1):
        p16 = [p_pow[i].astype(BF16) for i in n]
        p_pow = [_dot(p16[i], p16[i]) for i in n]
        fill(p_pow[0])
        t_upd = [_dot(t_inv[i].astype(BF16), p_pow[i].astype(BF16)) for i in n]
        fill(t_upd[0])
        t_inv = [t_inv[i] + t_upd[i] for i in n]
    half = base
    while half < size:
        lower_left = (rc_xor >= half) & (rc_xor < 2 * half)
        t16 = [t_inv[i].astype(BF16) for i in n]
        y = [_dot(jnp.where(lower_left, a_mats[i], 0.0).astype(BF16), t16[i]) for i in n]
        fill(y[0])
        t_upd = [_dot(t16[i], y[i].astype(BF16)) for i in n]
        fill(t_upd[0])
        t_inv = [t_inv[i] - t_upd[i] for i in n]
        half *= 2
    for thunk in fillers:
        thunk(_zero_after(t_inv[0]))
    return t_inv


def _gdn_body(qkv_ref, qkv_next_ref, z_ref, sm_ref, convw_ref, hist_ref, s0_ref, avec_ref, dtvec_ref, normw_ref,
              o_ref, sout_ref, state_ref, xpad_ref, act_ref, *, rows, chunk, lookahead):
    j = pl.program_id(1)
    cur = j % 2

    def conv_silu(lo, hi, zero_row=None):
        first = SUBLANES - (CONV_K - 1)
        w_first = convw_ref[0:1, lo:hi] if zero_row is None else convw_ref[0:1, lo:hi] + zero_row
        conv = w_first * xpad_ref[first:first + rows, lo:hi]
        for kk in range(1, CONV_K):
            conv = conv + convw_ref[kk:kk + 1, lo:hi] * xpad_ref[first + kk:first + kk + rows, lo:hi]
        return _silu(conv)

    def keep_history():
        xpad_ref[0:SUBLANES, :] = xpad_ref[rows:rows + SUBLANES, :]

    @pl.when(j == 0)
    def _():
        state_ref[...] = s0_ref[...]
        xpad_ref[0:SUBLANES, :] = hist_ref[...]
        xpad_ref[SUBLANES:SUBLANES + rows, :] = qkv_ref[...]
        act_ref[0] = conv_silu(0, GDN_QKV)
        keep_history()

    fillers = []
    if lookahead:
        def stage_next_block(zero_row):
            xpad_ref[SUBLANES:SUBLANES + rows, :] = qkv_next_ref[...]
        fillers.append(stage_next_block)
        for lo in range(0, GDN_QKV, GDN_D):
            def conv_slice(zero_row, lo=lo):
                act_ref[1 - cur, :, lo:lo + GDN_D] = conv_silu(lo, lo + GDN_D, zero_row)
            fillers.append(conv_slice)
        fillers.append(lambda zero_row: keep_history())

    sm = sm_ref[...]
    beta_all = _sigmoid(sm)
    g_all = avec_ref[...] * _softplus(sm + dtvec_ref[...])

    rr = lax.broadcasted_iota(jnp.int32, (rows, rows), 0)
    cc = lax.broadcasted_iota(jnp.int32, (rows, rows), 1)
    tri = jnp.where((rr >= cc) & ((rr ^ cc) < chunk), 1.0, 0.0).astype(BF16)
    g1, g2, g3 = _split3(g_all)
    gc_all = _dot(tri, g1) + _dot(tri, g2) + _dot(tri, g3)
    sel = jnp.where(lax.broadcasted_iota(jnp.int32, (SUBLANES, LANES), 1)
                    == lax.broadcasted_iota(jnp.int32, (SUBLANES, LANES), 0) + LANE_DECAY, 1.0, 0.0).astype(BF16)
    c1, c2, c3 = _split3(gc_all)
    gc_rows = _nt_dot(sel, c1) + _nt_dot(sel, c2) + _nt_dot(sel, c3)
    egc_all = jnp.exp(gc_all)
    normw = normw_ref[...]
    r_i = lax.broadcasted_iota(jnp.int32, (chunk, chunk), 0)
    c_i = lax.broadcasted_iota(jnp.int32, (chunk, chunk), 1)
    incl = r_i >= c_i
    strict = r_i > c_i

    heads = range(GDN_HEADS)
    pairs = [(c, h) for c in range(rows // chunk) for h in heads]
    gl_rows = [gc_all[(c + 1) * chunk - 1:(c + 1) * chunk, :] for c in range(rows // chunk)]
    ekg_all = [jnp.exp(gl_rows[c] - gc_all[c * chunk:(c + 1) * chunk, :]) for c in range(rows // chunk)]
    egl_all = [jnp.exp(gl_rows[c]) for c in range(rows // chunk)]
    qgs, kgs, rhss, a_mats, qks, egls = [], [], [], [], [], []
    for c, h in pairs:
        lo = c * chunk
        hi = lo + chunk
        q = act_ref[cur, lo:hi, h * GDN_D:(h + 1) * GDN_D]
        k = act_ref[cur, lo:hi, GDN_HD + h * GDN_D:GDN_HD + (h + 1) * GDN_D]
        v = act_ref[cur, lo:hi, 2 * GDN_HD + h * GDN_D:2 * GDN_HD + (h + 1) * GDN_D]
        q = q * lax.rsqrt(jnp.sum(q * q, axis=-1, keepdims=True) + EPS) * (GDN_D ** -0.5)
        k = k * lax.rsqrt(jnp.sum(k * k, axis=-1, keepdims=True) + EPS)
        beta = beta_all[lo:hi, LANE_BETA + h:LANE_BETA + h + 1]
        gcol = gc_all[lo:hi, LANE_DECAY + h:LANE_DECAY + h + 1]
        egc = egc_all[lo:hi, LANE_DECAY + h:LANE_DECAY + h + 1]
        grow = gc_rows[h:h + 1, lo:hi]
        kb = k * beta
        k16 = k.astype(BF16)
        decay = jnp.where(incl, jnp.exp(jnp.where(incl, gcol - grow, 0.0)), 0.0)
        qgs.append(q * egc)
        kgs.append((k * ekg_all[c][:, LANE_DECAY + h:LANE_DECAY + h + 1]).astype(BF16))
        egls.append(egl_all[c][:, LANE_DECAY + h:LANE_DECAY + h + 1])
        rhss.append(jnp.concatenate([v * beta, kb * egc], axis=1).astype(BF16))
        a_mats.append(jnp.where(strict, _nt_dot(kb.astype(BF16), k16) * decay, 0.0))
        qks.append(jnp.where(incl, _nt_dot(q.astype(BF16), k16) * decay, 0.0).astype(BF16))
    t_invs = _unit_lower_inverses(a_mats, r_i, c_i, chunk, fillers)
    uws = [_dot(t_invs[i].astype(BF16), rhss[i]) for i in range(len(pairs))]

    state = [state_ref[h] for h in heads]
    for c in range(rows // chunk):
        idx = [c * GDN_HEADS + h for h in heads]
        wss = [_dot(jnp.concatenate([uws[i][:, GDN_D:], qgs[i]], axis=0).astype(BF16), state[h].astype(BF16))
               for h, i in zip(heads, idx)]
        v_news = [(uws[i][:, :GDN_D] - ws[:chunk]).astype(BF16) for i, ws in zip(idx, wss)]
        outs = [ws[chunk:] + _dot(qks[i], v_new) for i, ws, v_new in zip(idx, wss, v_news)]
        state = [state[h] * egls[i] + _tn_dot(kgs[i], v_new) for h, i, v_new in zip(heads, idx, v_news)]
        for h in heads:
            zg = _silu(z_ref[c * chunk:(c + 1) * chunk, h * GDN_D:(h + 1) * GDN_D])
            o = outs[h]
            o = o * lax.rsqrt(jnp.mean(o * o, axis=-1, keepdims=True) + EPS) * normw * zg
            o_ref[c * chunk:(c + 1) * chunk, h * GDN_D:(h + 1) * GDN_D] = o.astype(o_ref.dtype)
    for h in heads:
        state_ref[h] = state[h]

    @pl.when(j == pl.num_programs(1) - 1)
    def _():
        sout_ref[0] = state_ref[...]


def _gdn(qkv, z, sm, convw, hist, s0, avec, dtvec, normw, *, batch, seq, rows, chunk):
    nblk = seq // rows
    row_spec = lambda n: pl.BlockSpec((rows, n), lambda b, j: (b * nblk + j, 0))
    next_spec = pl.BlockSpec((rows, GDN_QKV), lambda b, j: (b * nblk + jnp.minimum(j + 1, nblk - 1), 0))
    const2 = lambda a: pl.BlockSpec(a.shape, lambda b, j: (0, 0))
    body = functools.partial(_gdn_body, rows=rows, chunk=chunk, lookahead=nblk > 1)
    return pl.pallas_call(
        body,
        grid=(batch, nblk),
        in_specs=[row_spec(GDN_QKV), next_spec, row_spec(GDN_HD), row_spec(LANES), const2(convw), const2(hist),
                  pl.BlockSpec(s0.shape, lambda b, j: (0, 0, 0)), const2(avec), const2(dtvec), const2(normw)],
        out_specs=[row_spec(GDN_HD), pl.BlockSpec((1, GDN_HEADS, GDN_D, GDN_D), lambda b, j: (b, 0, 0, 0))],
        out_shape=[jax.ShapeDtypeStruct((batch * seq, GDN_HD), BF16),
                   jax.ShapeDtypeStruct((batch, GDN_HEADS, GDN_D, GDN_D), F32)],
        scratch_shapes=[pltpu.VMEM((GDN_HEADS, GDN_D, GDN_D), F32),
                        pltpu.VMEM((rows + SUBLANES, GDN_QKV), F32),
                        pltpu.VMEM((2, rows, GDN_QKV), F32)],
        compiler_params=pltpu.CompilerParams(dimension_semantics=("arbitrary", "arbitrary"),
                                             vmem_limit_bytes=VMEM_LIMIT),
        name="gdn",
    )(qkv, qkv, z, sm, convw, hist, s0, avec, dtvec, normw)


def _bias_constants():
    sel = np.zeros((LANES, 2 * FOX_HD), np.float32)
    ones = np.zeros((1, 2 * FOX_HD), np.float32)
    for head in range(FOX_HEADS):
        base = (head // 2) * LANES + BIAS_STRIDE * (head % 2)
        for t in range(3):
            src = LANE_FORGET + t * FOX_HEADS + head
            sel[src, base + t] = 1.0
            sel[src, FOX_HD + base + 3 + t] = -1.0
            ones[0, base + 3 + t] = 1.0
            ones[0, FOX_HD + base + t] = 1.0
    return jnp.asarray(sel, BF16), jnp.asarray(ones)


def _pack3(x):
    hi = x.astype(BF16).astype(F32)
    rest = x - hi
    mid = rest.astype(BF16).astype(F32)
    lo = (rest - mid).astype(BF16).astype(F32)
    return (hi + pltpu.roll(mid, FOX_HEADS, 1) + pltpu.roll(lo, 2 * FOX_HEADS, 1)).astype(BF16)


def _cumgate_body(sm_ref, fb_ref, c0_ref, sel_ref, ones_ref, ccol_ref, qb_ref, kb_ref, *, seq, blk):
    lane = lax.broadcasted_iota(jnp.int32, (1, LANES), 1)
    valid = (lane >= LANE_FORGET) & (lane < LANE_FORGET + FOX_HEADS)
    r_i = lax.broadcasted_iota(jnp.int32, (blk, blk), 0)
    c_i = lax.broadcasted_iota(jnp.int32, (blk, blk), 1)
    tri = jnp.where(r_i >= c_i, 1.0, 0.0).astype(BF16)
    carry = c0_ref[...]
    for i in range(seq // blk):
        xg = sm_ref[i * blk:(i + 1) * blk, :] + fb_ref[...]
        logf = jnp.where(valid, jnp.minimum(xg, 0.0) - jnp.log1p(jnp.exp(-jnp.abs(xg))), 0.0)
        cs3 = _dot(tri, _pack3(logf))
        cs = cs3 + pltpu.roll(cs3, LANES - FOX_HEADS, 1) + pltpu.roll(cs3, LANES - 2 * FOX_HEADS, 1)
        cs = jnp.where(valid, carry + cs, 0.0)
        ccol_ref[i * blk:(i + 1) * blk, :] = cs
        bias = _dot(_pack3(cs * LOG2E), sel_ref[...]) + ones_ref[...]
        qb_ref[i * blk:(i + 1) * blk, :] = bias[:, :FOX_HD].astype(BF16)
        kb_ref[i * blk:(i + 1) * blk, :] = bias[:, FOX_HD:].astype(BF16)
        carry = cs[blk - 1:blk, :]


def _cumgate(sm, fb, c0, consts, *, batch, seq):
    blk = min(256, seq)
    sel, ones = consts
    body = functools.partial(_cumgate_body, seq=seq, blk=blk)
    c2 = lambda a: pl.BlockSpec(a.shape, lambda b: (0, 0))
    return pl.pallas_call(
        body,
        grid=(batch,),
        in_specs=[pl.BlockSpec((seq, LANES), lambda b: (b, 0)), c2(fb), c2(c0), c2(sel), c2(ones)],
        out_specs=[pl.BlockSpec((seq, LANES), lambda b: (b, 0)), pl.BlockSpec((seq, FOX_HD), lambda b: (b, 0)),
                   pl.BlockSpec((seq, FOX_HD), lambda b: (b, 0))],
        out_shape=[jax.ShapeDtypeStruct((batch * seq, LANES), F32),
                   jax.ShapeDtypeStruct((batch * seq, FOX_HD), BF16),
                   jax.ShapeDtypeStruct((batch * seq, FOX_HD), BF16)],
        compiler_params=pltpu.CompilerParams(dimension_semantics=("arbitrary",), vmem_limit_bytes=VMEM_LIMIT),
        name="cumgate",
    )(sm, fb, c0, sel, ones)


def _fox_body(q_ref, qb_ref, k_ref, kb_ref, v_ref, km_ref, kbm_ref, vm_ref, o_ref,
              s_scr, p_scr, m_scr, alpha_scr, acc_scr, *, tq):
    i = pl.program_id(2)
    wide = tq
    lane = lax.broadcasted_iota(jnp.int32, (1, LANES), 1)
    q_all = q_ref[...]
    qb_all = qb_ref[...]
    zero = jnp.zeros_like(q_all)
    one = jnp.ones_like(q_all[0:1, :])
    own = [lane < FOX_DH, lane >= FOX_DH]
    q_aug = []
    for hh in range(2):
        bias_lanes = (lane >= BIAS_STRIDE * hh) & (lane < BIAS_STRIDE * (hh + 1))
        q_aug.append(jnp.concatenate([jnp.where(own[hh], q_all, zero), jnp.where(bias_lanes, qb_all, zero)], axis=1))

    def keys(start, width):
        return jnp.concatenate([k_ref[pl.ds(start, width), :], kb_ref[pl.ds(start, width), :]], axis=1)

    def values(start, width, hh):
        return jnp.where(own[hh], v_ref[pl.ds(start, width), :], one)

    n_full = i
    two = range(2)

    diag_start = pl.multiple_of(n_full * wide, wide)
    meta_mask = lax.broadcasted_iota(jnp.int32, (tq, LANES), 1) < N_META
    causal = lax.broadcasted_iota(jnp.int32, (tq, wide), 1) <= lax.broadcasted_iota(jnp.int32, (tq, wide), 0)
    km_aug = jnp.concatenate([km_ref[...], kbm_ref[...]], axis=1)
    kd_aug = keys(diag_start, wide)
    s_meta = [jnp.where(meta_mask, _nt_dot(q_aug[hh], km_aug), MASK_VALUE) for hh in two]
    s_diag = [jnp.where(causal, _nt_dot(q_aug[hh], kd_aug), MASK_VALUE) for hh in two]

    n_pairs = (n_full + 1) // 2

    def block_start(t):
        return pl.multiple_of(jnp.clip(t, 0, jnp.maximum(n_full - 1, 0)) * wide, wide)

    def value_start(t):
        return pl.multiple_of(jnp.where(t < 0, diag_start, block_start(t)), wide)

    def pipeline_step(u, carry):
        one_step(2 * u, 0, None)
        one_step(2 * u + 1, 1, 2 * u + 1 < n_full)
        return carry

    def one_step(t, cur, real_block):
        nxt = 1 - cur
        prev_start = value_start(t - 1)
        next_aug = keys(block_start(t + 1), wide)
        for hh in two:
            acc_scr[hh] = alpha_scr[hh] * acc_scr[hh] + _dot(p_scr[nxt, hh], values(prev_start, wide, hh))
        for hh in two:
            m_old = m_scr[hh]
            m_new = jnp.maximum(m_old, jnp.max(s_scr[cur, hh], axis=-1, keepdims=True))
            if real_block is not None:
                m_new = jnp.where(real_block, m_new, m_old)
            alpha_scr[hh] = jnp.exp2(m_old - m_new)
            m_scr[hh] = m_new
            p_scr[cur, hh] = jnp.exp2(s_scr[cur, hh] - m_new).astype(BF16)
        for hh in two:
            s_scr[nxt, hh] = _nt_dot(q_aug[hh], next_aug)

    first_aug = keys(block_start(0), wide)
    for hh in two:
        s_scr[0, hh] = _nt_dot(q_aug[hh], first_aug)
    for hh in two:
        m_first = jnp.maximum(jnp.max(s_meta[hh], axis=-1, keepdims=True),
                              jnp.max(s_diag[hh], axis=-1, keepdims=True))
        m_scr[hh] = m_first
        alpha_scr[hh] = jnp.ones((tq, 1), F32)
        p_scr[1, hh] = jnp.exp2(s_diag[hh] - m_first).astype(BF16)
        acc_scr[hh] = _dot(jnp.exp2(s_meta[hh] - m_first).astype(BF16), jnp.where(own[hh], vm_ref[...], one))
    lax.fori_loop(0, n_pairs, pipeline_step, 0)

    t_last = 2 * n_pairs - 1
    last_start = value_start(t_last)
    keep_last = jnp.where(t_last < n_full, 1.0, 0.0)
    outs = []
    for hh in two:
        acc = alpha_scr[hh] * acc_scr[hh] + keep_last * _dot(p_scr[1, hh], values(last_start, wide, hh))
        outs.append(acc / pltpu.roll(acc, FOX_DH, 1))
    o_ref[...] = jnp.where(own[0], outs[0], outs[1]).astype(o_ref.dtype)


def _fox(fq, qb, fk, kb, fv, km, kbm, vm, *, batch, seq, tq):
    nq = seq // tq
    npair = FOX_HEADS // 2
    body = functools.partial(_fox_body, tq=tq)
    qspec = pl.BlockSpec((tq, LANES), lambda b, p, i: (b * nq + i, p))
    kspec = pl.BlockSpec((seq, LANES), lambda b, p, i: (b, p))
    mspec = pl.BlockSpec((LANES, LANES), lambda b, p, i: (0, p))
    return pl.pallas_call(
        body,
        grid=(batch, npair, nq),
        in_specs=[qspec, qspec, kspec, kspec, kspec, mspec, mspec, mspec],
        out_specs=qspec,
        out_shape=jax.ShapeDtypeStruct((batch * seq, FOX_HD), BF16),
        scratch_shapes=[pltpu.VMEM((2, 2, tq, tq), F32), pltpu.VMEM((2, 2, tq, tq), BF16),
                        pltpu.VMEM((2, tq, 1), F32), pltpu.VMEM((2, tq, 1), F32), pltpu.VMEM((2, tq, LANES), F32)],
        compiler_params=pltpu.CompilerParams(dimension_semantics=("arbitrary", "arbitrary", "arbitrary"),
                                             vmem_limit_bytes=VMEM_LIMIT),
        name="fox",
    )(fq, qb, fk, kb, fv, km, kbm, vm)


def _ffn_body(x_ref, og_ref, of_ref, wo_ref, nw_ref, wg_ref, wu_ref, wd_ref, fw_ref, out_ref, act_ref, *, fchunk):
    h1 = x_ref[...] + _dot(og_ref[...], wo_ref[0:GDN_HD, :]) + _dot(of_ref[...], wo_ref[GDN_HD:GDN_HD + FOX_HD, :])
    n = (h1 * lax.rsqrt(jnp.mean(h1 * h1, axis=-1, keepdims=True) + EPS) * nw_ref[...]).astype(BF16)
    d_ff = wg_ref.shape[1]
    for c in range(d_ff // fchunk):
        sl = slice(c * fchunk, (c + 1) * fchunk)
        g = _dot(n, wg_ref[:, sl])
        up = _dot(n, wu_ref[:, sl])
        act_ref[:, sl] = (_silu(g) * up).astype(BF16)
    acc = h1 + _dot(act_ref[...], wd_ref[...])
    out_ref[...] = acc * lax.rsqrt(jnp.mean(acc * acc, axis=-1, keepdims=True) + EPS) * fw_ref[...]


def _ffn(x2d, og, of, wo, nw, wg, wu, wd, fw, tm):
    rows = x2d.shape[0]
    row_spec = lambda n: pl.BlockSpec((tm, n), lambda i: (i, 0))
    const_spec = lambda a: pl.BlockSpec(a.shape, lambda i: (0, 0), pipeline_mode=pl.Buffered(1))
    body = functools.partial(_ffn_body, fchunk=256)
    return pl.pallas_call(
        body,
        grid=(rows // tm,),
        in_specs=[row_spec(D_MODEL), row_spec(GDN_HD), row_spec(FOX_HD), const_spec(wo), const_spec(nw),
                  const_spec(wg), const_spec(wu), const_spec(wd), const_spec(fw)],
        out_specs=row_spec(D_MODEL),
        out_shape=jax.ShapeDtypeStruct((rows, D_MODEL), F32),
        scratch_shapes=[pltpu.VMEM((tm, wg.shape[1]), BF16)],
        compiler_params=pltpu.CompilerParams(dimension_semantics=("arbitrary",), vmem_limit_bytes=VMEM_LIMIT),
        name="ffn",
    )(x2d, og, of, wo, nw, wg, wu, wd, fw)


def _lane_vec(vals, offset):
    return jnp.zeros((1, LANES), F32).at[0, offset:offset + vals.shape[0]].set(vals.astype(F32))


def kernel(x, meta_tokens, attn_norm_w, w_in, conv_w, a_log, dt_bias, gdn_norm_w, fgate_b, w_out, ffn_norm_w,
           w_gate, w_up, w_down, final_norm_w):
    batch, seq, _ = x.shape
    depth = w_in.shape[0]
    assert depth == 1 and seq % 512 == 0
    w = w_in[0]
    s = [0, 512, 1024, 1536, 2048, 2052, 2056, 2568, 3080, 3592, 3600]
    wbig = jnp.concatenate([w[:, s[0]:s[4]], w[:, s[6]:s[9]]], axis=1).astype(BF16)
    wsm = jnp.concatenate([w[:, s[4]:s[6]], w[:, s[9]:s[10]],
                           jnp.zeros((D_MODEL, LANES - 2 * GDN_HEADS - FOX_HEADS), F32)], axis=1).astype(BF16)
    anw = attn_norm_w[0][None, :]
    avec = _lane_vec(-jnp.exp(a_log[0]), LANE_DECAY)
    dtvec = _lane_vec(dt_bias[0], LANE_DECAY)
    fbvec = _lane_vec(fgate_b[0], LANE_FORGET)
    gnw = gdn_norm_w[0][None, :]
    convw = conv_w[0]
    bias_consts = _bias_constants()

    x2d = x.reshape(batch * seq, D_MODEL)

    qkv_m, z_m, _, fk_m, fv_m, sm_m = _inproj(meta_tokens, anw, wbig, wsm, tm=N_META)
    zeros_hist = jnp.zeros((SUBLANES, GDN_QKV), F32)
    zeros_state = jnp.zeros((GDN_HEADS, GDN_D, GDN_D), F32)
    _, s_meta = _gdn(qkv_m, z_m, sm_m, convw, zeros_hist, zeros_state, avec, dtvec, gnw,
                     batch=1, seq=N_META, rows=N_META, chunk=N_META)
    ccol_m, _, kb_m = _cumgate(sm_m, fbvec, jnp.zeros((1, LANES), F32), bias_consts, batch=1, seq=N_META)
    pad_rows = ((0, LANES - N_META), (0, 0))
    km = jnp.pad(fk_m, pad_rows)
    kbm = jnp.pad(kb_m, pad_rows)
    vm = jnp.pad(fv_m, pad_rows)

    qkv, z, fq, fk, fv, sm = _inproj(x2d, anw, wbig, wsm, tm=512)
    o_gdn, _ = _gdn(qkv, z, sm, convw, qkv_m[N_META - SUBLANES:], s_meta[0], avec, dtvec, gnw,
                    batch=batch, seq=seq, rows=256, chunk=128)
    _, qb, kb = _cumgate(sm, fbvec, ccol_m[N_META - 1:], bias_consts, batch=batch, seq=seq)
    o_fox = _fox(fq, qb, fk, kb, fv, km, kbm, vm, batch=batch, seq=seq, tq=512)

    out = _ffn(x2d, o_gdn, o_fox, w_out[0].astype(BF16), ffn_norm_w[0][None, :], w_gate[0].astype(BF16),
               w_up[0].astype(BF16), w_down[0].astype(BF16), final_norm_w[None, :], tm=512)
    return out.reshape(batch, seq, D_MODEL)
```

```python
import functools
import math

import numpy as np
import jax
import jax.numpy as jnp
from jax import lax
from jax.experimental import pallas as pl
from jax.experimental.pallas import tpu as pltpu

F32 = jnp.float32
BF16 = jnp.bfloat16

D_MODEL = 1024
N_META = 16
GDN_HEADS = 4
GDN_D = 128
FOX_HEADS = 8
FOX_DH = 64
CONV_K = 4
EPS = 1e-6
MASK_VALUE = -1e30
LOG2E = 1.4426950408889634
GDN_QKV = 3 * GDN_HEADS * GDN_D
GDN_HD = GDN_HEADS * GDN_D
FOX_HD = FOX_HEADS * FOX_DH
LANES = 128
SUBLANES = 8
LANE_BETA = 0
LANE_DECAY = GDN_HEADS
LANE_FORGET = 2 * GDN_HEADS
BIAS_STRIDE = 8
INV_BASE = 16
CONV_SLICE = 256
VMEM_LIMIT = 56 * 1024 * 1024


def _nt_dot(a, b):
    return lax.dot_general(a, b, (((1,), (1,)), ((), ())), preferred_element_type=F32)


def _tn_dot(a, b):
    return lax.dot_general(a, b, (((0,), (0,)), ((), ())), preferred_element_type=F32)


def _dot(a, b):
    return jnp.dot(a, b, preferred_element_type=F32)


def _split3(x):
    x1 = x.astype(BF16)
    r1 = x - x1.astype(F32)
    x2 = r1.astype(BF16)
    x3 = (r1 - x2.astype(F32)).astype(BF16)
    return x1, x2, x3


def _softplus(x):
    return jnp.maximum(x, 0.0) + jnp.log1p(jnp.exp(-jnp.abs(x)))


def _sigmoid(x):
    return 1.0 / (1.0 + jnp.exp(-x))


def _silu(x):
    return x * _sigmoid(x)


def _inproj_body(x_ref, nw_ref, wbig_ref, wsm_ref, convw_ref, hist_ref,
                 act_ref, z_ref, fq_ref, fk_ref, fv_ref, sm_ref, tail_ref, xpad_ref, *, tm, tiles_per_seq):
    @pl.when(pl.program_id(0) % tiles_per_seq == 0)
    def _():
        xpad_ref[0:SUBLANES, :] = hist_ref[...]

    x = x_ref[...]
    u = (x * lax.rsqrt(jnp.mean(x * x, axis=-1, keepdims=True) + EPS) * nw_ref[...]).astype(BF16)
    o0 = GDN_QKV
    o1 = o0 + GDN_HD
    o2 = o1 + FOX_HD
    o3 = o2 + FOX_HD
    o4 = o3 + FOX_HD

    def proj_z():
        z_ref[...] = _dot(u, wbig_ref[:, o0:o1])

    def proj_fq():
        fq_ref[...] = (_dot(u, wbig_ref[:, o1:o2]) * (LOG2E * FOX_DH ** -0.5)).astype(BF16)

    def proj_fk():
        fk_ref[...] = _dot(u, wbig_ref[:, o2:o3]).astype(BF16)

    def proj_fv():
        fv_ref[...] = _dot(u, wbig_ref[:, o3:o4]).astype(BF16)

    def proj_sm():
        sm_ref[...] = _dot(u, wsm_ref[...])

    others = iter([proj_z, proj_fq, proj_fk, proj_fv, proj_sm])
    first = SUBLANES - (CONV_K - 1)
    for lo in range(0, GDN_QKV, CONV_SLICE):
        hi = lo + CONV_SLICE
        xpad_ref[SUBLANES:SUBLANES + tm, lo:hi] = _dot(u, wbig_ref[:, lo:hi])
        nxt = next(others, None)
        if nxt is not None:
            nxt()
        conv = convw_ref[0:1, lo:hi] * xpad_ref[first:first + tm, lo:hi]
        for kk in range(1, CONV_K):
            conv = conv + convw_ref[kk:kk + 1, lo:hi] * xpad_ref[first + kk:first + kk + tm, lo:hi]
        act_ref[:, lo:hi] = _silu(conv)
    for nxt in others:
        nxt()
    tail_ref[...] = xpad_ref[tm:tm + SUBLANES, :]
    xpad_ref[0:SUBLANES, :] = xpad_ref[tm:tm + SUBLANES, :]


def _inproj(x2d, nw, wbig, wsm, convw, hist, *, tm, tiles_per_seq):
    rows = x2d.shape[0]
    ntiles = rows // tm
    row_spec = lambda n: pl.BlockSpec((tm, n), lambda i: (i, 0))
    const_spec = lambda a: pl.BlockSpec(a.shape, lambda i: (0, 0))
    body = functools.partial(_inproj_body, tm=tm, tiles_per_seq=tiles_per_seq)
    return pl.pallas_call(
        body,
        grid=(ntiles,),
        in_specs=[row_spec(D_MODEL), const_spec(nw), const_spec(wbig), const_spec(wsm), const_spec(convw),
                  const_spec(hist)],
        out_specs=[row_spec(GDN_QKV), row_spec(GDN_HD), row_spec(FOX_HD), row_spec(FOX_HD), row_spec(FOX_HD),
                   row_spec(LANES), pl.BlockSpec((SUBLANES, GDN_QKV), lambda i: (i, 0))],
        out_shape=[jax.ShapeDtypeStruct((rows, GDN_QKV), F32), jax.ShapeDtypeStruct((rows, GDN_HD), F32),
                   jax.ShapeDtypeStruct((rows, FOX_HD), BF16), jax.ShapeDtypeStruct((rows, FOX_HD), BF16),
                   jax.ShapeDtypeStruct((rows, FOX_HD), BF16), jax.ShapeDtypeStruct((rows, LANES), F32),
                   jax.ShapeDtypeStruct((ntiles * SUBLANES, GDN_QKV), F32)],
        scratch_shapes=[pltpu.VMEM((tm + SUBLANES, GDN_QKV), F32)],
        compiler_params=pltpu.CompilerParams(dimension_semantics=("arbitrary",), vmem_limit_bytes=VMEM_LIMIT),
        name="inproj",
    )(x2d, nw, wbig, wsm, convw, hist)


def _unit_lower_inverses(a_mats, r_i, c_i, size):
    n = range(len(a_mats))
    base = min(INV_BASE, size)
    eye = jnp.where(r_i == c_i, 1.0, 0.0).astype(F32)
    rc_xor = r_i ^ c_i
    p_pow = [jnp.where(rc_xor < base, -a_mats[i], 0.0) for i in n]
    t_inv = [eye + p_pow[i] for i in n]
    for _ in range(int(math.log2(base)) - 1):
        p16 = [p_pow[i].astype(BF16) for i in n]
        p_pow = [_dot(p16[i], p16[i]) for i in n]
        t_inv = [t_inv[i] + _dot(t_inv[i].astype(BF16), p_pow[i].astype(BF16)) for i in n]
    half = base
    while half < size:
        lower_left = (rc_xor >= half) & (rc_xor < 2 * half)
        t16 = [t_inv[i].astype(BF16) for i in n]
        y16 = [_dot(jnp.where(lower_left, a_mats[i], 0.0).astype(BF16), t16[i]).astype(BF16) for i in n]
        t_inv = [t_inv[i] - _dot(t16[i], y16[i]) for i in n]
        half *= 2
    return t_inv


def _gdn_body(act_ref, z_ref, sm_ref, s0_ref, avec_ref, dtvec_ref, normw_ref, o_ref, sout_ref, state_ref,
              *, rows, chunk):
    j = pl.program_id(1)

    @pl.when(j == 0)
    def _():
        state_ref[...] = s0_ref[...]

    sm = sm_ref[...]
    beta_all = _sigmoid(sm)
    g_all = avec_ref[...] * _softplus(sm + dtvec_ref[...])

    rr = lax.broadcasted_iota(jnp.int32, (rows, rows), 0)
    cc = lax.broadcasted_iota(jnp.int32, (rows, rows), 1)
    tri = jnp.where((rr >= cc) & ((rr ^ cc) < chunk), 1.0, 0.0).astype(BF16)
    g1, g2, g3 = _split3(g_all)
    gc_all = _dot(tri, g1) + _dot(tri, g2) + _dot(tri, g3)
    sel = jnp.where(lax.broadcasted_iota(jnp.int32, (SUBLANES, LANES), 1)
                    == lax.broadcasted_iota(jnp.int32, (SUBLANES, LANES), 0) + LANE_DECAY, 1.0, 0.0).astype(BF16)
    c1, c2, c3 = _split3(gc_all)
    gc_rows = _nt_dot(sel, c1) + _nt_dot(sel, c2) + _nt_dot(sel, c3)
    egc_all = jnp.exp(gc_all)
    normw = normw_ref[...]
    r_i = lax.broadcasted_iota(jnp.int32, (chunk, chunk), 0)
    c_i = lax.broadcasted_iota(jnp.int32, (chunk, chunk), 1)
    incl = r_i >= c_i
    strict = r_i > c_i

    heads = range(GDN_HEADS)
    pairs = [(c, h) for c in range(rows // chunk) for h in heads]
    gl_rows = [gc_all[(c + 1) * chunk - 1:(c + 1) * chunk, :] for c in range(rows // chunk)]
    ekg_all = [jnp.exp(gl_rows[c] - gc_all[c * chunk:(c + 1) * chunk, :]) for c in range(rows // chunk)]
    egl_all = [jnp.exp(gl_rows[c]) for c in range(rows // chunk)]
    qgs, kgs, rhss, a_mats, qks, egls = [], [], [], [], [], []
    for c, h in pairs:
        lo = c * chunk
        hi = lo + chunk
        q = act_ref[lo:hi, h * GDN_D:(h + 1) * GDN_D]
        k = act_ref[lo:hi, GDN_HD + h * GDN_D:GDN_HD + (h + 1) * GDN_D]
        v = act_ref[lo:hi, 2 * GDN_HD + h * GDN_D:2 * GDN_HD + (h + 1) * GDN_D]
        q = q * lax.rsqrt(jnp.sum(q * q, axis=-1, keepdims=True) + EPS) * (GDN_D ** -0.5)
        k = k * lax.rsqrt(jnp.sum(k * k, axis=-1, keepdims=True) + EPS)
        beta = beta_all[lo:hi, LANE_BETA + h:LANE_BETA + h + 1]
        gcol = gc_all[lo:hi, LANE_DECAY + h:LANE_DECAY + h + 1]
        egc = egc_all[lo:hi, LANE_DECAY + h:LANE_DECAY + h + 1]
        grow = gc_rows[h:h + 1, lo:hi]
        kb = k * beta
        k16 = k.astype(BF16)
        decay = jnp.where(incl, jnp.exp(jnp.where(incl, gcol - grow, 0.0)), 0.0)
        qgs.append(q * egc)
        kgs.append((k * ekg_all[c][:, LANE_DECAY + h:LANE_DECAY + h + 1]).astype(BF16))
        egls.append(egl_all[c][:, LANE_DECAY + h:LANE_DECAY + h + 1])
        rhss.append(jnp.concatenate([v * beta, kb * egc], axis=1).astype(BF16))
        a_mats.append(jnp.where(strict, _nt_dot(kb.astype(BF16), k16) * decay, 0.0))
        qks.append(jnp.where(incl, _nt_dot(q.astype(BF16), k16) * decay, 0.0).astype(BF16))
    t_invs = _unit_lower_inverses(a_mats, r_i, c_i, chunk)
    uws = [_dot(t_invs[i].astype(BF16), rhss[i]) for i in range(len(pairs))]

    state = [state_ref[h] for h in heads]
    for c in range(rows // chunk):
        idx = [c * GDN_HEADS + h for h in heads]
        wss = [_dot(jnp.concatenate([uws[i][:, GDN_D:], qgs[i]], axis=0).astype(BF16), state[h].astype(BF16))
               for h, i in zip(heads, idx)]
        v_news = [(uws[i][:, :GDN_D] - ws[:chunk]).astype(BF16) for i, ws in zip(idx, wss)]
        outs = [ws[chunk:] + _dot(qks[i], v_new) for i, ws, v_new in zip(idx, wss, v_news)]
        state = [state[h] * egls[i] + _tn_dot(kgs[i], v_new) for h, i, v_new in zip(heads, idx, v_news)]
        for h in heads:
            zg = _silu(z_ref[c * chunk:(c + 1) * chunk, h * GDN_D:(h + 1) * GDN_D])
            o = outs[h]
            o = o * lax.rsqrt(jnp.mean(o * o, axis=-1, keepdims=True) + EPS) * normw * zg
            o_ref[c * chunk:(c + 1) * chunk, h * GDN_D:(h + 1) * GDN_D] = o.astype(o_ref.dtype)
    for h in heads:
        state_ref[h] = state[h]

    @pl.when(j == pl.num_programs(1) - 1)
    def _():
        sout_ref[0] = state_ref[...]


def _gdn(act, z, sm, s0, avec, dtvec, normw, *, batch, seq, rows, chunk):
    nblk = seq // rows
    row_spec = lambda n: pl.BlockSpec((rows, n), lambda b, j: (b * nblk + j, 0))
    const2 = lambda a: pl.BlockSpec(a.shape, lambda b, j: (0, 0))
    body = functools.partial(_gdn_body, rows=rows, chunk=chunk)
    return pl.pallas_call(
        body,
        grid=(batch, nblk),
        in_specs=[row_spec(GDN_QKV), row_spec(GDN_HD), row_spec(LANES),
                  pl.BlockSpec(s0.shape, lambda b, j: (0, 0, 0)), const2(avec), const2(dtvec), const2(normw)],
        out_specs=[row_spec(GDN_HD), pl.BlockSpec((1, GDN_HEADS, GDN_D, GDN_D), lambda b, j: (b, 0, 0, 0))],
        out_shape=[jax.ShapeDtypeStruct((batch * seq, GDN_HD), BF16),
                   jax.ShapeDtypeStruct((batch, GDN_HEADS, GDN_D, GDN_D), F32)],
        scratch_shapes=[pltpu.VMEM((GDN_HEADS, GDN_D, GDN_D), F32)],
        compiler_params=pltpu.CompilerParams(dimension_semantics=("arbitrary", "arbitrary"),
                                             vmem_limit_bytes=VMEM_LIMIT),
        name="gdn",
    )(act, z, sm, s0, avec, dtvec, normw)


def _bias_constants():
    sel = np.zeros((LANES, 2 * FOX_HD), np.float32)
    ones = np.zeros((1, 2 * FOX_HD), np.float32)
    for head in range(FOX_HEADS):
        base = (head // 2) * LANES + BIAS_STRIDE * (head % 2)
        for t in range(3):
            src = LANE_FORGET + t * FOX_HEADS + head
            sel[src, base + t] = 1.0
            sel[src, FOX_HD + base + 3 + t] = -1.0
            ones[0, base + 3 + t] = 1.0
            ones[0, FOX_HD + base + t] = 1.0
    return jnp.asarray(sel, BF16), jnp.asarray(ones)


def _pack3(x):
    hi = x.astype(BF16).astype(F32)
    rest = x - hi
    mid = rest.astype(BF16).astype(F32)
    lo = (rest - mid).astype(BF16).astype(F32)
    return (hi + pltpu.roll(mid, FOX_HEADS, 1) + pltpu.roll(lo, 2 * FOX_HEADS, 1)).astype(BF16)


def _cumgate_body(sm_ref, fb_ref, c0_ref, sel_ref, ones_ref, ccol_ref, qb_ref, kb_ref, *, seq, blk):
    lane = lax.broadcasted_iota(jnp.int32, (1, LANES), 1)
    valid = (lane >= LANE_FORGET) & (lane < LANE_FORGET + FOX_HEADS)
    r_i = lax.broadcasted_iota(jnp.int32, (blk, blk), 0)
    c_i = lax.broadcasted_iota(jnp.int32, (blk, blk), 1)
    tri = jnp.where(r_i >= c_i, 1.0, 0.0).astype(BF16)
    carry = c0_ref[...]
    for i in range(seq // blk):
        xg = sm_ref[i * blk:(i + 1) * blk, :] + fb_ref[...]
        logf = jnp.where(valid, jnp.minimum(xg, 0.0) - jnp.log1p(jnp.exp(-jnp.abs(xg))), 0.0)
        cs3 = _dot(tri, _pack3(logf))
        cs = cs3 + pltpu.roll(cs3, LANES - FOX_HEADS, 1) + pltpu.roll(cs3, LANES - 2 * FOX_HEADS, 1)
        cs = jnp.where(valid, carry + cs, 0.0)
        ccol_ref[i * blk:(i + 1) * blk, :] = cs
        bias = _dot(_pack3(cs * LOG2E), sel_ref[...]) + ones_ref[...]
        qb_ref[i * blk:(i + 1) * blk, :] = bias[:, :FOX_HD].astype(BF16)
        kb_ref[i * blk:(i + 1) * blk, :] = bias[:, FOX_HD:].astype(BF16)
        carry = cs[blk - 1:blk, :]


def _cumgate(sm, fb, c0, consts, *, batch, seq):
    blk = min(256, seq)
    sel, ones = consts
    body = functools.partial(_cumgate_body, seq=seq, blk=blk)
    c2 = lambda a: pl.BlockSpec(a.shape, lambda b: (0, 0))
    return pl.pallas_call(
        body,
        grid=(batch,),
        in_specs=[pl.BlockSpec((seq, LANES), lambda b: (b, 0)), c2(fb), c2(c0), c2(sel), c2(ones)],
        out_specs=[pl.BlockSpec((seq, LANES), lambda b: (b, 0)), pl.BlockSpec((seq, FOX_HD), lambda b: (b, 0)),
                   pl.BlockSpec((seq, FOX_HD), lambda b: (b, 0))],
        out_shape=[jax.ShapeDtypeStruct((batch * seq, LANES), F32),
                   jax.ShapeDtypeStruct((batch * seq, FOX_HD), BF16),
                   jax.ShapeDtypeStruct((batch * seq, FOX_HD), BF16)],
        compiler_params=pltpu.CompilerParams(dimension_semantics=("arbitrary",), vmem_limit_bytes=VMEM_LIMIT),
        name="cumgate",
    )(sm, fb, c0, sel, ones)


def _fox_body(q_ref, qb_ref, k_ref, kb_ref, v_ref, km_ref, kbm_ref, vm_ref, o_ref,
              s_scr, p_scr, m_scr, alpha_scr, acc_scr, *, tq):
    i = pl.program_id(2)
    wide = tq
    lane = lax.broadcasted_iota(jnp.int32, (1, LANES), 1)
    q_all = q_ref[...]
    qb_all = qb_ref[...]
    zero = jnp.zeros_like(q_all)
    one = jnp.ones_like(q_all[0:1, :])
    own = [lane < FOX_DH, lane >= FOX_DH]
    q_aug = []
    for hh in range(2):
        bias_lanes = (lane >= BIAS_STRIDE * hh) & (lane < BIAS_STRIDE * (hh + 1))
        q_aug.append(jnp.concatenate([jnp.where(own[hh], q_all, zero), jnp.where(bias_lanes, qb_all, zero)], axis=1))

    def keys(start, width):
        return jnp.concatenate([k_ref[pl.ds(start, width), :], kb_ref[pl.ds(start, width), :]], axis=1)

    def values(start, width, hh):
        return jnp.where(own[hh], v_ref[pl.ds(start, width), :], one)

    n_full = i
    two = range(2)

    diag_start = pl.multiple_of(n_full * wide, wide)
    meta_mask = lax.broadcasted_iota(jnp.int32, (tq, LANES), 1) < N_META
    causal = lax.broadcasted_iota(jnp.int32, (tq, wide), 1) <= lax.broadcasted_iota(jnp.int32, (tq, wide), 0)
    km_aug = jnp.concatenate([km_ref[...], kbm_ref[...]], axis=1)
    kd_aug = keys(diag_start, wide)
    s_meta = [jnp.where(meta_mask, _nt_dot(q_aug[hh], km_aug), MASK_VALUE) for hh in two]
    s_diag = [jnp.where(causal, _nt_dot(q_aug[hh], kd_aug), MASK_VALUE) for hh in two]

    n_pairs = (n_full + 1) // 2

    def block_start(t):
        return pl.multiple_of(jnp.clip(t, 0, jnp.maximum(n_full - 1, 0)) * wide, wide)

    def value_start(t):
        return pl.multiple_of(jnp.where(t < 0, diag_start, block_start(t)), wide)

    def pipeline_step(u, carry):
        one_step(2 * u, 0, None)
        one_step(2 * u + 1, 1, 2 * u + 1 < n_full)
        return carry

    def one_step(t, cur, real_block):
        nxt = 1 - cur
        prev_start = value_start(t - 1)
        next_aug = keys(block_start(t + 1), wide)
        for hh in two:
            acc_scr[hh] = alpha_scr[hh] * acc_scr[hh] + _dot(p_scr[nxt, hh], values(prev_start, wide, hh))
        for hh in two:
            m_old = m_scr[hh]
            m_new = jnp.maximum(m_old, jnp.max(s_scr[cur, hh], axis=-1, keepdims=True))
            if real_block is not None:
                m_new = jnp.where(real_block, m_new, m_old)
            alpha_scr[hh] = jnp.exp2(m_old - m_new)
            m_scr[hh] = m_new
            p_scr[cur, hh] = jnp.exp2(s_scr[cur, hh] - m_new).astype(BF16)
        for hh in two:
            s_scr[nxt, hh] = _nt_dot(q_aug[hh], next_aug)

    first_aug = keys(block_start(0), wide)
    for hh in two:
        s_scr[0, hh] = _nt_dot(q_aug[hh], first_aug)
    for hh in two:
        m_first = jnp.maximum(jnp.max(s_meta[hh], axis=-1, keepdims=True),
                              jnp.max(s_diag[hh], axis=-1, keepdims=True))
        m_scr[hh] = m_first
        alpha_scr[hh] = jnp.ones((tq, 1), F32)
        p_scr[1, hh] = jnp.exp2(s_diag[hh] - m_first).astype(BF16)
        acc_scr[hh] = _dot(jnp.exp2(s_meta[hh] - m_first).astype(BF16), jnp.where(own[hh], vm_ref[...], one))
    lax.fori_loop(0, n_pairs, pipeline_step, 0)

    t_last = 2 * n_pairs - 1
    last_start = value_start(t_last)
    keep_last = jnp.where(t_last < n_full, 1.0, 0.0)
    outs = []
    for hh in two:
        acc = alpha_scr[hh] * acc_scr[hh] + keep_last * _dot(p_scr[1, hh], values(last_start, wide, hh))
        outs.append(acc / pltpu.roll(acc, FOX_DH, 1))
    o_ref[...] = jnp.where(own[0], outs[0], outs[1]).astype(o_ref.dtype)


def _fox(fq, qb, fk, kb, fv, km, kbm, vm, *, batch, seq, tq):
    nq = seq // tq
    npair = FOX_HEADS // 2
    body = functools.partial(_fox_body, tq=tq)
    qspec = pl.BlockSpec((tq, LANES), lambda b, p, i: (b * nq + i, p))
    kspec = pl.BlockSpec((seq, LANES), lambda b, p, i: (b, p))
    mspec = pl.BlockSpec((LANES, LANES), lambda b, p, i: (0, p))
    return pl.pallas_call(
        body,
        grid=(batch, npair, nq),
        in_specs=[qspec, qspec, kspec, kspec, kspec, mspec, mspec, mspec],
        out_specs=qspec,
        out_shape=jax.ShapeDtypeStruct((batch * seq, FOX_HD), BF16),
        scratch_shapes=[pltpu.VMEM((2, 2, tq, tq), F32), pltpu.VMEM((2, 2, tq, tq), BF16),
                        pltpu.VMEM((2, tq, 1), F32), pltpu.VMEM((2, tq, 1), F32), pltpu.VMEM((2, tq, LANES), F32)],
        compiler_params=pltpu.CompilerParams(dimension_semantics=("arbitrary", "arbitrary", "arbitrary"),
                                             vmem_limit_bytes=VMEM_LIMIT),
        name="fox",
    )(fq, qb, fk, kb, fv, km, kbm, vm)


def _ffn_body(x_ref, og_ref, of_ref, wo_ref, nw_ref, wg_ref, wu_ref, wd_ref, fw_ref, out_ref, act_ref, *, fchunk):
    h1 = x_ref[...] + _dot(og_ref[...], wo_ref[0:GDN_HD, :]) + _dot(of_ref[...], wo_ref[GDN_HD:GDN_HD + FOX_HD, :])
    n = (h1 * lax.rsqrt(jnp.mean(h1 * h1, axis=-1, keepdims=True) + EPS) * nw_ref[...]).astype(BF16)
    d_ff = wg_ref.shape[1]
    for c in range(d_ff // fchunk):
        sl = slice(c * fchunk, (c + 1) * fchunk)
        g = _dot(n, wg_ref[:, sl])
        up = _dot(n, wu_ref[:, sl])
        act_ref[:, sl] = (_silu(g) * up).astype(BF16)
    acc = h1 + _dot(act_ref[...], wd_ref[...])
    out_ref[...] = acc * lax.rsqrt(jnp.mean(acc * acc, axis=-1, keepdims=True) + EPS) * fw_ref[...]


def _ffn(x2d, og, of, wo, nw, wg, wu, wd, fw, tm):
    rows = x2d.shape[0]
    row_spec = lambda n: pl.BlockSpec((tm, n), lambda i: (i, 0))
    const_spec = lambda a: pl.BlockSpec(a.shape, lambda i: (0, 0), pipeline_mode=pl.Buffered(1))
    body = functools.partial(_ffn_body, fchunk=256)
    return pl.pallas_call(
        body,
        grid=(rows // tm,),
        in_specs=[row_spec(D_MODEL), row_spec(GDN_HD), row_spec(FOX_HD), const_spec(wo), const_spec(nw),
                  const_spec(wg), const_spec(wu), const_spec(wd), const_spec(fw)],
        out_specs=row_spec(D_MODEL),
        out_shape=jax.ShapeDtypeStruct((rows, D_MODEL), F32),
        scratch_shapes=[pltpu.VMEM((tm, wg.shape[1]), BF16)],
        compiler_params=pltpu.CompilerParams(dimension_semantics=("arbitrary",), vmem_limit_bytes=VMEM_LIMIT),
        name="ffn",
    )(x2d, og, of, wo, nw, wg, wu, wd, fw)


def _lane_vec(vals, offset):
    return jnp.zeros((1, LANES), F32).at[0, offset:offset + vals.shape[0]].set(vals.astype(F32))


def kernel(x, meta_tokens, attn_norm_w, w_in, conv_w, a_log, dt_bias, gdn_norm_w, fgate_b, w_out, ffn_norm_w,
           w_gate, w_up, w_down, final_norm_w):
    batch, seq, _ = x.shape
    depth = w_in.shape[0]
    assert depth == 1 and seq % 512 == 0
    w = w_in[0]
    s = [0, 512, 1024, 1536, 2048, 2052, 2056, 2568, 3080, 3592, 3600]
    wbig = jnp.concatenate([w[:, s[0]:s[4]], w[:, s[6]:s[9]]], axis=1).astype(BF16)
    wsm = jnp.concatenate([w[:, s[4]:s[6]], w[:, s[9]:s[10]],
                           jnp.zeros((D_MODEL, LANES - 2 * GDN_HEADS - FOX_HEADS), F32)], axis=1).astype(BF16)
    anw = attn_norm_w[0][None, :]
    avec = _lane_vec(-jnp.exp(a_log[0]), LANE_DECAY)
    dtvec = _lane_vec(dt_bias[0], LANE_DECAY)
    fbvec = _lane_vec(fgate_b[0], LANE_FORGET)
    gnw = gdn_norm_w[0][None, :]
    convw = conv_w[0]
    bias_consts = _bias_constants()

    x2d = x.reshape(batch * seq, D_MODEL)

    zeros_hist = jnp.zeros((SUBLANES, GDN_QKV), F32)
    act_m, z_m, _, fk_m, fv_m, sm_m, tail_m = _inproj(meta_tokens, anw, wbig, wsm, convw, zeros_hist,
                                                      tm=N_META, tiles_per_seq=1)
    zeros_state = jnp.zeros((GDN_HEADS, GDN_D, GDN_D), F32)
    _, s_meta = _gdn(act_m, z_m, sm_m, zeros_state, avec, dtvec, gnw,
                     batch=1, seq=N_META, rows=N_META, chunk=N_META)
    ccol_m, _, kb_m = _cumgate(sm_m, fbvec, jnp.zeros((1, LANES), F32), bias_consts, batch=1, seq=N_META)
    pad_rows = ((0, LANES - N_META), (0, 0))
    km = jnp.pad(fk_m, pad_rows)
    kbm = jnp.pad(kb_m, pad_rows)
    vm = jnp.pad(fv_m, pad_rows)

    act, z, fq, fk, fv, sm, _ = _inproj(x2d, anw, wbig, wsm, convw, tail_m, tm=512, tiles_per_seq=seq // 512)
    o_gdn, _ = _gdn(act, z, sm, s_meta[0], avec, dtvec, gnw, batch=batch, seq=seq, rows=256, chunk=128)
    _, qb, kb = _cumgate(sm, fbvec, ccol_m[N_META - 1:], bias_consts, batch=batch, seq=seq)
    o_fox = _fox(fq, qb, fk, kb, fv, km, kbm, vm, batch=batch, seq=seq, tq=512)

    out = _ffn(x2d, o_gdn, o_fox, w_out[0].astype(BF16), ffn_norm_w[0][None, :], w_gate[0].astype(BF16),
               w_up[0].astype(BF16), w_down[0].astype(BF16), final_norm_w[None, :], tm=512)
    return out.reshape(batch, seq, D_MODEL)
```

```python
import functools
import math

import numpy as np
import jax
import jax.numpy as jnp
from jax import lax
from jax.experimental import pallas as pl
from jax.experimental.pallas import tpu as pltpu

F32 = jnp.float32
BF16 = jnp.bfloat16

D_MODEL = 1024
N_META = 16
GDN_HEADS = 4
GDN_D = 128
FOX_HEADS = 8
FOX_DH = 64
CONV_K = 4
EPS = 1e-6
MASK_VALUE = -1e30
LOG2E = 1.4426950408889634
GDN_QKV = 3 * GDN_HEADS * GDN_D
GDN_HD = GDN_HEADS * GDN_D
FOX_HD = FOX_HEADS * FOX_DH
LANES = 128
SUBLANES = 8
LANE_BETA = 0
LANE_DECAY = GDN_HEADS
LANE_FORGET = 2 * GDN_HEADS
BIAS_STRIDE = 8
INV_BASE = 16
CONV_SLICE = 512
VMEM_LIMIT = 56 * 1024 * 1024


def _nt_dot(a, b):
    return lax.dot_general(a, b, (((1,), (1,)), ((), ())), preferred_element_type=F32)


def _tn_dot(a, b):
    return lax.dot_general(a, b, (((0,), (0,)), ((), ())), preferred_element_type=F32)


def _dot(a, b):
    return jnp.dot(a, b, preferred_element_type=F32)


def _split3(x):
    x1 = x.astype(BF16)
    r1 = x - x1.astype(F32)
    x2 = r1.astype(BF16)
    x3 = (r1 - x2.astype(F32)).astype(BF16)
    return x1, x2, x3


def _softplus(x):
    return jnp.maximum(x, 0.0) + jnp.log1p(jnp.exp(-jnp.abs(x)))


def _sigmoid(x):
    return 1.0 / (1.0 + jnp.exp(-x))


def _silu(x):
    return x * _sigmoid(x)


def _inproj_body(x_ref, nw_ref, wbig_ref, wsm_ref, convw_ref, hist_ref,
                 act_ref, z_ref, fq_ref, fk_ref, fv_ref, sm_ref, tail_ref, xpad_ref, *, tm, tiles_per_seq):
    @pl.when(pl.program_id(0) % tiles_per_seq == 0)
    def _():
        xpad_ref[0:SUBLANES, :] = hist_ref[...]

    x = x_ref[...]
    u = (x * lax.rsqrt(jnp.mean(x * x, axis=-1, keepdims=True) + EPS) * nw_ref[...]).astype(BF16)
    o0 = GDN_QKV
    o1 = o0 + GDN_HD
    o2 = o1 + FOX_HD
    o3 = o2 + FOX_HD
    o4 = o3 + FOX_HD

    def proj_z():
        z_ref[...] = _dot(u, wbig_ref[:, o0:o1])

    def proj_fq():
        fq_ref[...] = (_dot(u, wbig_ref[:, o1:o2]) * (LOG2E * FOX_DH ** -0.5)).astype(BF16)

    def proj_fk():
        fk_ref[...] = _dot(u, wbig_ref[:, o2:o3]).astype(BF16)

    def proj_fv():
        fv_ref[...] = _dot(u, wbig_ref[:, o3:o4]).astype(BF16)

    def proj_sm():
        sm_ref[...] = _dot(u, wsm_ref[...])

    others = iter([proj_z, proj_fq, proj_fk, proj_fv, proj_sm])
    first = SUBLANES - (CONV_K - 1)
    for lo in range(0, GDN_QKV, CONV_SLICE):
        hi = lo + CONV_SLICE
        xpad_ref[SUBLANES:SUBLANES + tm, lo:hi] = _dot(u, wbig_ref[:, lo:hi])
        nxt = next(others, None)
        if nxt is not None:
            nxt()
        conv = convw_ref[0:1, lo:hi] * xpad_ref[first:first + tm, lo:hi]
        for kk in range(1, CONV_K):
            conv = conv + convw_ref[kk:kk + 1, lo:hi] * xpad_ref[first + kk:first + kk + tm, lo:hi]
        act_ref[:, lo:hi] = _silu(conv)
    for nxt in others:
        nxt()
    tail_ref[...] = xpad_ref[tm:tm + SUBLANES, :]
    xpad_ref[0:SUBLANES, :] = xpad_ref[tm:tm + SUBLANES, :]


def _inproj(x2d, nw, wbig, wsm, convw, hist, *, tm, tiles_per_seq):
    rows = x2d.shape[0]
    ntiles = rows // tm
    row_spec = lambda n: pl.BlockSpec((tm, n), lambda i: (i, 0))
    const_spec = lambda a: pl.BlockSpec(a.shape, lambda i: (0, 0))
    body = functools.partial(_inproj_body, tm=tm, tiles_per_seq=tiles_per_seq)
    return pl.pallas_call(
        body,
        grid=(ntiles,),
        in_specs=[row_spec(D_MODEL), const_spec(nw), const_spec(wbig), const_spec(wsm), const_spec(convw),
                  const_spec(hist)],
        out_specs=[row_spec(GDN_QKV), row_spec(GDN_HD), row_spec(FOX_HD), row_spec(FOX_HD), row_spec(FOX_HD),
                   row_spec(LANES), pl.BlockSpec((SUBLANES, GDN_QKV), lambda i: (i, 0))],
        out_shape=[jax.ShapeDtypeStruct((rows, GDN_QKV), F32), jax.ShapeDtypeStruct((rows, GDN_HD), F32),
                   jax.ShapeDtypeStruct((rows, FOX_HD), BF16), jax.ShapeDtypeStruct((rows, FOX_HD), BF16),
                   jax.ShapeDtypeStruct((rows, FOX_HD), BF16), jax.ShapeDtypeStruct((rows, LANES), F32),
                   jax.ShapeDtypeStruct((ntiles * SUBLANES, GDN_QKV), F32)],
        scratch_shapes=[pltpu.VMEM((tm + SUBLANES, GDN_QKV), F32)],
        compiler_params=pltpu.CompilerParams(dimension_semantics=("arbitrary",), vmem_limit_bytes=VMEM_LIMIT),
        name="inproj",
    )(x2d, nw, wbig, wsm, convw, hist)


def _unit_lower_inverses(a_mats, r_i, c_i, size):
    n = range(len(a_mats))
    base = min(INV_BASE, size)
    eye = jnp.where(r_i == c_i, 1.0, 0.0).astype(F32)
    rc_xor = r_i ^ c_i
    p_pow = [jnp.where(rc_xor < base, -a_mats[i], 0.0) for i in n]
    t_inv = [eye + p_pow[i] for i in n]
    for _ in range(int(math.log2(base)) - 1):
        p16 = [p_pow[i].astype(BF16) for i in n]
        p_pow = [_dot(p16[i], p16[i]) for i in n]
        t_inv = [t_inv[i] + _dot(t_inv[i].astype(BF16), p_pow[i].astype(BF16)) for i in n]
    half = base
    while half < size:
        lower_left = (rc_xor >= half) & (rc_xor < 2 * half)
        t16 = [t_inv[i].astype(BF16) for i in n]
        y16 = [_dot(jnp.where(lower_left, a_mats[i], 0.0).astype(BF16), t16[i]).astype(BF16) for i in n]
        t_inv = [t_inv[i] - _dot(t16[i], y16[i]) for i in n]
        half *= 2
    return t_inv


def _gdn_body(act_ref, z_ref, sm_ref, s0_ref, avec_ref, dtvec_ref, normw_ref, o_ref, sout_ref, state_ref,
              *, rows, chunk):
    j = pl.program_id(1)

    @pl.when(j == 0)
    def _():
        state_ref[...] = s0_ref[...]

    sm = sm_ref[...]
    beta_all = _sigmoid(sm)
    g_all = avec_ref[...] * _softplus(sm + dtvec_ref[...])

    rr = lax.broadcasted_iota(jnp.int32, (rows, rows), 0)
    cc = lax.broadcasted_iota(jnp.int32, (rows, rows), 1)
    tri = jnp.where((rr >= cc) & ((rr ^ cc) < chunk), 1.0, 0.0).astype(BF16)
    g1, g2, g3 = _split3(g_all)
    gc_all = _dot(tri, g1) + _dot(tri, g2) + _dot(tri, g3)
    sel = jnp.where(lax.broadcasted_iota(jnp.int32, (SUBLANES, LANES), 1)
                    == lax.broadcasted_iota(jnp.int32, (SUBLANES, LANES), 0) + LANE_DECAY, 1.0, 0.0).astype(BF16)
    c1, c2, c3 = _split3(gc_all)
    gc_rows = _nt_dot(sel, c1) + _nt_dot(sel, c2) + _nt_dot(sel, c3)
    egc_all = jnp.exp(gc_all)
    normw = normw_ref[...]
    r_i = lax.broadcasted_iota(jnp.int32, (chunk, chunk), 0)
    c_i = lax.broadcasted_iota(jnp.int32, (chunk, chunk), 1)
    incl = r_i >= c_i
    strict = r_i > c_i

    heads = range(GDN_HEADS)
    pairs = [(c, h) for c in range(rows // chunk) for h in heads]
    gl_rows = [gc_all[(c + 1) * chunk - 1:(c + 1) * chunk, :] for c in range(rows // chunk)]
    ekg_all = [jnp.exp(gl_rows[c] - gc_all[c * chunk:(c + 1) * chunk, :]) for c in range(rows // chunk)]
    egl_all = [jnp.exp(gl_rows[c]) for c in range(rows // chunk)]
    qgs, kgs, rhss, a_mats, qks, egls = [], [], [], [], [], []
    for c, h in pairs:
        lo = c * chunk
        hi = lo + chunk
        q = act_ref[lo:hi, h * GDN_D:(h + 1) * GDN_D]
        k = act_ref[lo:hi, GDN_HD + h * GDN_D:GDN_HD + (h + 1) * GDN_D]
        v = act_ref[lo:hi, 2 * GDN_HD + h * GDN_D:2 * GDN_HD + (h + 1) * GDN_D]
        q = q * lax.rsqrt(jnp.sum(q * q, axis=-1, keepdims=True) + EPS) * (GDN_D ** -0.5)
        k = k * lax.rsqrt(jnp.sum(k * k, axis=-1, keepdims=True) + EPS)
        beta = beta_all[lo:hi, LANE_BETA + h:LANE_BETA + h + 1]
        gcol = gc_all[lo:hi, LANE_DECAY + h:LANE_DECAY + h + 1]
        egc = egc_all[lo:hi, LANE_DECAY + h:LANE_DECAY + h + 1]
        grow = gc_rows[h:h + 1, lo:hi]
        kb = k * beta
        k16 = k.astype(BF16)
        decay = jnp.where(incl, jnp.exp(jnp.where(incl, gcol - grow, 0.0)), 0.0)
        qgs.append(q * egc)
        kgs.append((k * ekg_all[c][:, LANE_DECAY + h:LANE_DECAY + h + 1]).astype(BF16))
        egls.append(egl_all[c][:, LANE_DECAY + h:LANE_DECAY + h + 1])
        rhss.append(jnp.concatenate([v * beta, kb * egc], axis=1).astype(BF16))
        a_mats.append(jnp.where(strict, _nt_dot(kb.astype(BF16), k16) * decay, 0.0))
        qks.append(jnp.where(incl, _nt_dot(q.astype(BF16), k16) * decay, 0.0).astype(BF16))
    t_invs = _unit_lower_inverses(a_mats, r_i, c_i, chunk)
    uws = [_dot(t_invs[i].astype(BF16), rhss[i]) for i in range(len(pairs))]

    state = [state_ref[h] for h in heads]
    for c in range(rows // chunk):
        idx = [c * GDN_HEADS + h for h in heads]
        wss = [_dot(jnp.concatenate([uws[i][:, GDN_D:], qgs[i]], axis=0).astype(BF16), state[h].astype(BF16))
               for h, i in zip(heads, idx)]
        v_news = [(uws[i][:, :GDN_D] - ws[:chunk]).astype(BF16) for i, ws in zip(idx, wss)]
        outs = [ws[chunk:] + _dot(qks[i], v_new) for i, ws, v_new in zip(idx, wss, v_news)]
        state = [state[h] * egls[i] + _tn_dot(kgs[i], v_new) for h, i, v_new in zip(heads, idx, v_news)]
        for h in heads:
            zg = _silu(z_ref[c * chunk:(c + 1) * chunk, h * GDN_D:(h + 1) * GDN_D])
            o = outs[h]
            o = o * lax.rsqrt(jnp.mean(o * o, axis=-1, keepdims=True) + EPS) * normw * zg
            o_ref[c * chunk:(c + 1) * chunk, h * GDN_D:(h + 1) * GDN_D] = o.astype(o_ref.dtype)
    for h in heads:
        state_ref[h] = state[h]

    @pl.when(j == pl.num_programs(1) - 1)
    def _():
        sout_ref[0] = state_ref[...]


def _gdn(act, z, sm, s0, avec, dtvec, normw, *, batch, seq, rows, chunk):
    nblk = seq // rows
    row_spec = lambda n: pl.BlockSpec((rows, n), lambda b, j: (b * nblk + j, 0))
    const2 = lambda a: pl.BlockSpec(a.shape, lambda b, j: (0, 0))
    body = functools.partial(_gdn_body, rows=rows, chunk=chunk)
    return pl.pallas_call(
        body,
        grid=(batch, nblk),
        in_specs=[row_spec(GDN_QKV), row_spec(GDN_HD), row_spec(LANES),
                  pl.BlockSpec(s0.shape, lambda b, j: (0, 0, 0)), const2(avec), const2(dtvec), const2(normw)],
        out_specs=[row_spec(GDN_HD), pl.BlockSpec((1, GDN_HEADS, GDN_D, GDN_D), lambda b, j: (b, 0, 0, 0))],
        out_shape=[jax.ShapeDtypeStruct((batch * seq, GDN_HD), BF16),
                   jax.ShapeDtypeStruct((batch, GDN_HEADS, GDN_D, GDN_D), F32)],
        scratch_shapes=[pltpu.VMEM((GDN_HEADS, GDN_D, GDN_D), F32)],
        compiler_params=pltpu.CompilerParams(dimension_semantics=("arbitrary", "arbitrary"),
                                             vmem_limit_bytes=VMEM_LIMIT),
        name="gdn",
    )(act, z, sm, s0, avec, dtvec, normw)


def _bias_constants():
    sel = np.zeros((LANES, 2 * FOX_HD), np.float32)
    ones = np.zeros((1, 2 * FOX_HD), np.float32)
    for head in range(FOX_HEADS):
        base = (head // 2) * LANES + BIAS_STRIDE * (head % 2)
        for t in range(3):
            src = LANE_FORGET + t * FOX_HEADS + head
            sel[src, base + t] = 1.0
            sel[src, FOX_HD + base + 3 + t] = -1.0
            ones[0, base + 3 + t] = 1.0
            ones[0, FOX_HD + base + t] = 1.0
    return jnp.asarray(sel, BF16), jnp.asarray(ones)


def _pack3(x):
    hi = x.astype(BF16).astype(F32)
    rest = x - hi
    mid = rest.astype(BF16).astype(F32)
    lo = (rest - mid).astype(BF16).astype(F32)
    return (hi + pltpu.roll(mid, FOX_HEADS, 1) + pltpu.roll(lo, 2 * FOX_HEADS, 1)).astype(BF16)


def _cumgate_body(sm_ref, fb_ref, c0_ref, sel_ref, ones_ref, ccol_ref, qb_ref, kb_ref, *, seq, blk):
    lane = lax.broadcasted_iota(jnp.int32, (1, LANES), 1)
    valid = (lane >= LANE_FORGET) & (lane < LANE_FORGET + FOX_HEADS)
    r_i = lax.broadcasted_iota(jnp.int32, (blk, blk), 0)
    c_i = lax.broadcasted_iota(jnp.int32, (blk, blk), 1)
    tri = jnp.where(r_i >= c_i, 1.0, 0.0).astype(BF16)
    carry = c0_ref[...]
    for i in range(seq // blk):
        xg = sm_ref[i * blk:(i + 1) * blk, :] + fb_ref[...]
        logf = jnp.where(valid, jnp.minimum(xg, 0.0) - jnp.log1p(jnp.exp(-jnp.abs(xg))), 0.0)
        cs3 = _dot(tri, _pack3(logf))
        cs = cs3 + pltpu.roll(cs3, LANES - FOX_HEADS, 1) + pltpu.roll(cs3, LANES - 2 * FOX_HEADS, 1)
        cs = jnp.where(valid, carry + cs, 0.0)
        ccol_ref[i * blk:(i + 1) * blk, :] = cs
        bias = _dot(_pack3(cs * LOG2E), sel_ref[...]) + ones_ref[...]
        qb_ref[i * blk:(i + 1) * blk, :] = bias[:, :FOX_HD].astype(BF16)
        kb_ref[i * blk:(i + 1) * blk, :] = bias[:, FOX_HD:].astype(BF16)
        carry = cs[blk - 1:blk, :]


def _cumgate(sm, fb, c0, consts, *, batch, seq):
    blk = min(256, seq)
    sel, ones = consts
    body = functools.partial(_cumgate_body, seq=seq, blk=blk)
    c2 = lambda a: pl.BlockSpec(a.shape, lambda b: (0, 0))
    return pl.pallas_call(
        body,
        grid=(batch,),
        in_specs=[pl.BlockSpec((seq, LANES), lambda b: (b, 0)), c2(fb), c2(c0), c2(sel), c2(ones)],
        out_specs=[pl.BlockSpec((seq, LANES), lambda b: (b, 0)), pl.BlockSpec((seq, FOX_HD), lambda b: (b, 0)),
                   pl.BlockSpec((seq, FOX_HD), lambda b: (b, 0))],
        out_shape=[jax.ShapeDtypeStruct((batch * seq, LANES), F32),
                   jax.ShapeDtypeStruct((batch * seq, FOX_HD), BF16),
                   jax.ShapeDtypeStruct((batch * seq, FOX_HD), BF16)],
        compiler_params=pltpu.CompilerParams(dimension_semantics=("arbitrary",), vmem_limit_bytes=VMEM_LIMIT),
        name="cumgate",
    )(sm, fb, c0, sel, ones)


def _fox_body(q_ref, qb_ref, k_ref, kb_ref, v_ref, km_ref, kbm_ref, vm_ref, o_ref,
              s_scr, p_scr, m_scr, alpha_scr, acc_scr, *, tq):
    i = pl.program_id(2)
    wide = tq
    lane = lax.broadcasted_iota(jnp.int32, (1, LANES), 1)
    q_all = q_ref[...]
    qb_all = qb_ref[...]
    zero = jnp.zeros_like(q_all)
    one = jnp.ones_like(q_all[0:1, :])
    own = [lane < FOX_DH, lane >= FOX_DH]
    q_aug = []
    for hh in range(2):
        bias_lanes = (lane >= BIAS_STRIDE * hh) & (lane < BIAS_STRIDE * (hh + 1))
        q_aug.append(jnp.concatenate([jnp.where(own[hh], q_all, zero), jnp.where(bias_lanes, qb_all, zero)], axis=1))

    def keys(start, width):
        return jnp.concatenate([k_ref[pl.ds(start, width), :], kb_ref[pl.ds(start, width), :]], axis=1)

    def values(start, width, hh):
        return jnp.where(own[hh], v_ref[pl.ds(start, width), :], one)

    n_full = i
    two = range(2)

    diag_start = pl.multiple_of(n_full * wide, wide)
    meta_mask = lax.broadcasted_iota(jnp.int32, (tq, LANES), 1) < N_META
    causal = lax.broadcasted_iota(jnp.int32, (tq, wide), 1) <= lax.broadcasted_iota(jnp.int32, (tq, wide), 0)
    km_aug = jnp.concatenate([km_ref[...], kbm_ref[...]], axis=1)
    kd_aug = keys(diag_start, wide)
    s_meta = [jnp.where(meta_mask, _nt_dot(q_aug[hh], km_aug), MASK_VALUE) for hh in two]
    s_diag = [jnp.where(causal, _nt_dot(q_aug[hh], kd_aug), MASK_VALUE) for hh in two]

    n_pairs = (n_full + 1) // 2

    def block_start(t):
        return pl.multiple_of(jnp.clip(t, 0, jnp.maximum(n_full - 1, 0)) * wide, wide)

    def value_start(t):
        return pl.multiple_of(jnp.where(t < 0, diag_start, block_start(t)), wide)

    def pipeline_step(u, carry):
        one_step(2 * u, 0, None)
        one_step(2 * u + 1, 1, 2 * u + 1 < n_full)
        return carry

    def one_step(t, cur, real_block):
        nxt = 1 - cur
        prev_start = value_start(t - 1)
        next_aug = keys(block_start(t + 1), wide)
        for hh in two:
            acc_scr[hh] = alpha_scr[hh] * acc_scr[hh] + _dot(p_scr[nxt, hh], values(prev_start, wide, hh))
        for hh in two:
            m_old = m_scr[hh]
            m_new = jnp.maximum(m_old, jnp.max(s_scr[cur, hh], axis=-1, keepdims=True))
            if real_block is not None:
                m_new = jnp.where(real_block, m_new, m_old)
            alpha_scr[hh] = jnp.exp2(m_old - m_new)
            m_scr[hh] = m_new
            p_scr[cur, hh] = jnp.exp2(s_scr[cur, hh] - m_new).astype(BF16)
        for hh in two:
            s_scr[nxt, hh] = _nt_dot(q_aug[hh], next_aug)

    first_aug = keys(block_start(0), wide)
    for hh in two:
        s_scr[0, hh] = _nt_dot(q_aug[hh], first_aug)
    for hh in two:
        m_first = jnp.maximum(jnp.max(s_meta[hh], axis=-1, keepdims=True),
                              jnp.max(s_diag[hh], axis=-1, keepdims=True))
        m_scr[hh] = m_first
        alpha_scr[hh] = jnp.ones((tq, 1), F32)
        p_scr[1, hh] = jnp.exp2(s_diag[hh] - m_first).astype(BF16)
        acc_scr[hh] = _dot(jnp.exp2(s_meta[hh] - m_first).astype(BF16), jnp.where(own[hh], vm_ref[...], one))
    lax.fori_loop(0, n_pairs, pipeline_step, 0)

    t_last = 2 * n_pairs - 1
    last_start = value_start(t_last)
    keep_last = jnp.where(t_last < n_full, 1.0, 0.0)
    outs = []
    for hh in two:
        acc = alpha_scr[hh] * acc_scr[hh] + keep_last * _dot(p_scr[1, hh], values(last_start, wide, hh))
        outs.append(acc / pltpu.roll(acc, FOX_DH, 1))
    o_ref[...] = jnp.where(own[0], outs[0], outs[1]).astype(o_ref.dtype)


def _fox(fq, qb, fk, kb, fv, km, kbm, vm, *, batch, seq, tq):
    nq = seq // tq
    npair = FOX_HEADS // 2
    body = functools.partial(_fox_body, tq=tq)
    qspec = pl.BlockSpec((tq, LANES), lambda b, p, i: (b * nq + i, p))
    kspec = pl.BlockSpec((seq, LANES), lambda b, p, i: (b, p))
    mspec = pl.BlockSpec((LANES, LANES), lambda b, p, i: (0, p))
    return pl.pallas_call(
        body,
        grid=(batch, npair, nq),
        in_specs=[qspec, qspec, kspec, kspec, kspec, mspec, mspec, mspec],
        out_specs=qspec,
        out_shape=jax.ShapeDtypeStruct((batch * seq, FOX_HD), BF16),
        scratch_shapes=[pltpu.VMEM((2, 2, tq, tq), F32), pltpu.VMEM((2, 2, tq, tq), BF16),
                        pltpu.VMEM((2, tq, 1), F32), pltpu.VMEM((2, tq, 1), F32), pltpu.VMEM((2, tq, LANES), F32)],
        compiler_params=pltpu.CompilerParams(dimension_semantics=("arbitrary", "arbitrary", "arbitrary"),
                                             vmem_limit_bytes=VMEM_LIMIT),
        name="fox",
    )(fq, qb, fk, kb, fv, km, kbm, vm)


def _ffn_body(x_ref, og_ref, of_ref, wo_ref, nw_ref, wg_ref, wu_ref, wd_ref, fw_ref, out_ref, act_ref, *, fchunk):
    h1 = x_ref[...] + _dot(og_ref[...], wo_ref[0:GDN_HD, :]) + _dot(of_ref[...], wo_ref[GDN_HD:GDN_HD + FOX_HD, :])
    n = (h1 * lax.rsqrt(jnp.mean(h1 * h1, axis=-1, keepdims=True) + EPS) * nw_ref[...]).astype(BF16)
    d_ff = wg_ref.shape[1]
    for c in range(d_ff // fchunk):
        sl = slice(c * fchunk, (c + 1) * fchunk)
        g = _dot(n, wg_ref[:, sl])
        up = _dot(n, wu_ref[:, sl])
        act_ref[:, sl] = (_silu(g) * up).astype(BF16)
    acc = h1 + _dot(act_ref[...], wd_ref[...])
    out_ref[...] = acc * lax.rsqrt(jnp.mean(acc * acc, axis=-1, keepdims=True) + EPS) * fw_ref[...]


def _ffn(x2d, og, of, wo, nw, wg, wu, wd, fw, tm):
    rows = x2d.shape[0]
    row_spec = lambda n: pl.BlockSpec((tm, n), lambda i: (i, 0))
    const_spec = lambda a: pl.BlockSpec(a.shape, lambda i: (0, 0), pipeline_mode=pl.Buffered(1))
    body = functools.partial(_ffn_body, fchunk=256)
    return pl.pallas_call(
        body,
        grid=(rows // tm,),
        in_specs=[row_spec(D_MODEL), row_spec(GDN_HD), row_spec(FOX_HD), const_spec(wo), const_spec(nw),
                  const_spec(wg), const_spec(wu), const_spec(wd), const_spec(fw)],
        out_specs=row_spec(D_MODEL),
        out_shape=jax.ShapeDtypeStruct((rows, D_MODEL), F32),
        scratch_shapes=[pltpu.VMEM((tm, wg.shape[1]), BF16)],
        compiler_params=pltpu.CompilerParams(dimension_semantics=("arbitrary",), vmem_limit_bytes=VMEM_LIMIT),
        name="ffn",
    )(x2d, og, of, wo, nw, wg, wu, wd, fw)


def _lane_vec(vals, offset):
    return jnp.zeros((1, LANES), F32).at[0, offset:offset + vals.shape[0]].set(vals.astype(F32))


def kernel(x, meta_tokens, attn_norm_w, w_in, conv_w, a_log, dt_bias, gdn_norm_w, fgate_b, w_out, ffn_norm_w,
           w_gate, w_up, w_down, final_norm_w):
    batch, seq, _ = x.shape
    depth = w_in.shape[0]
    assert depth == 1 and seq % 512 == 0
    w = w_in[0]
    s = [0, 512, 1024, 1536, 2048, 2052, 2056, 2568, 3080, 3592, 3600]
    wbig = jnp.concatenate([w[:, s[0]:s[4]], w[:, s[6]:s[9]]], axis=1).astype(BF16)
    wsm = jnp.concatenate([w[:, s[4]:s[6]], w[:, s[9]:s[10]],
                           jnp.zeros((D_MODEL, LANES - 2 * GDN_HEADS - FOX_HEADS), F32)], axis=1).astype(BF16)
    anw = attn_norm_w[0][None, :]
    avec = _lane_vec(-jnp.exp(a_log[0]), LANE_DECAY)
    dtvec = _lane_vec(dt_bias[0], LANE_DECAY)
    fbvec = _lane_vec(fgate_b[0], LANE_FORGET)
    gnw = gdn_norm_w[0][None, :]
    convw = conv_w[0]
    bias_consts = _bias_constants()

    x2d = x.reshape(batch * seq, D_MODEL)

    zeros_hist = jnp.zeros((SUBLANES, GDN_QKV), F32)
    act_m, z_m, _, fk_m, fv_m, sm_m, tail_m = _inproj(meta_tokens, anw, wbig, wsm, convw, zeros_hist,
                                                      tm=N_META, tiles_per_seq=1)
    zeros_state = jnp.zeros((GDN_HEADS, GDN_D, GDN_D), F32)
    _, s_meta = _gdn(act_m, z_m, sm_m, zeros_state, avec, dtvec, gnw,
                     batch=1, seq=N_META, rows=N_META, chunk=N_META)
    ccol_m, _, kb_m = _cumgate(sm_m, fbvec, jnp.zeros((1, LANES), F32), bias_consts, batch=1, seq=N_META)
    pad_rows = ((0, LANES - N_META), (0, 0))
    km = jnp.pad(fk_m, pad_rows)
    kbm = jnp.pad(kb_m, pad_rows)
    vm = jnp.pad(fv_m, pad_rows)

    act, z, fq, fk, fv, sm, _ = _inproj(x2d, anw, wbig, wsm, convw, tail_m, tm=512, tiles_per_seq=seq // 512)
    o_gdn, _ = _gdn(act, z, sm, s_meta[0], avec, dtvec, gnw, batch=batch, seq=seq, rows=512, chunk=128)
    _, qb, kb = _cumgate(sm, fbvec, ccol_m[N_META - 1:], bias_consts, batch=batch, seq=seq)
    o_fox = _fox(fq, qb, fk, kb, fv, km, kbm, vm, batch=batch, seq=seq, tq=512)

    out = _ffn(x2d, o_gdn, o_fox, w_out[0].astype(BF16), ffn_norm_w[0][None, :], w_gate[0].astype(BF16),
               w_up[0].astype(BF16), w_down[0].astype(BF16), final_norm_w[None, :], tm=512)
    return out.reshape(batch, seq, D_MODEL)
```

```python
import functools
import math

import numpy as np
import jax
import jax.numpy as jnp
from jax import lax
from jax.experimental import pallas as pl
from jax.experimental.pallas import tpu as pltpu

F32 = jnp.float32
BF16 = jnp.bfloat16

D_MODEL = 1024
N_META = 16
GDN_HEADS = 4
GDN_D = 128
FOX_HEADS = 8
FOX_DH = 64
CONV_K = 4
EPS = 1e-6
MASK_VALUE = -1e30
LOG2E = 1.4426950408889634
GDN_QKV = 3 * GDN_HEADS * GDN_D
GDN_HD = GDN_HEADS * GDN_D
FOX_HD = FOX_HEADS * FOX_DH
LANES = 128
SUBLANES = 8
LANE_BETA = 0
LANE_DECAY = GDN_HEADS
LANE_FORGET = 2 * GDN_HEADS
BIAS_STRIDE = 8
INV_BASE = 16
CONV_SLICE = 512
VMEM_LIMIT = 56 * 1024 * 1024
INPROJ_ROWS = 512
GDN_ROWS = 512
GDN_CHUNK = 128
CUMGATE_ROWS = 256
FOX_ROWS = 512
FFN_ROWS = 512
FFN_COLS = 256


def _nt_dot(a, b):
    return lax.dot_general(a, b, (((1,), (1,)), ((), ())), preferred_element_type=F32)


def _tn_dot(a, b):
    return lax.dot_general(a, b, (((0,), (0,)), ((), ())), preferred_element_type=F32)


def _dot(a, b):
    return jnp.dot(a, b, preferred_element_type=F32)


def _split3(x):
    x1 = x.astype(BF16)
    r1 = x - x1.astype(F32)
    x2 = r1.astype(BF16)
    x3 = (r1 - x2.astype(F32)).astype(BF16)
    return x1, x2, x3


def _softplus(x):
    return jnp.maximum(x, 0.0) + jnp.log1p(jnp.exp(-jnp.abs(x)))


def _sigmoid(x):
    return 1.0 / (1.0 + jnp.exp(-x))


def _silu(x):
    return x * _sigmoid(x)


def _inproj_body(x_ref, nw_ref, wbig_ref, wsm_ref, convw_ref, hist_ref,
                 act_ref, z_ref, fq_ref, fk_ref, fv_ref, sm_ref, tail_ref, xpad_ref, *, tm, tiles_per_seq):
    @pl.when(pl.program_id(0) % tiles_per_seq == 0)
    def _():
        xpad_ref[0:SUBLANES, :] = hist_ref[...]

    x = x_ref[...]
    u = (x * lax.rsqrt(jnp.mean(x * x, axis=-1, keepdims=True) + EPS) * nw_ref[...]).astype(BF16)
    o0 = GDN_QKV
    o1 = o0 + GDN_HD
    o2 = o1 + FOX_HD
    o3 = o2 + FOX_HD
    o4 = o3 + FOX_HD

    def proj_z():
        z_ref[...] = _dot(u, wbig_ref[:, o0:o1])

    def proj_fq():
        fq_ref[...] = (_dot(u, wbig_ref[:, o1:o2]) * (LOG2E * FOX_DH ** -0.5)).astype(BF16)

    def proj_fk():
        fk_ref[...] = _dot(u, wbig_ref[:, o2:o3]).astype(BF16)

    def proj_fv():
        fv_ref[...] = _dot(u, wbig_ref[:, o3:o4]).astype(BF16)

    def proj_sm():
        sm_ref[...] = _dot(u, wsm_ref[...])

    others = iter([proj_z, proj_fq, proj_fk, proj_fv, proj_sm])
    first = SUBLANES - (CONV_K - 1)
    for lo in range(0, GDN_QKV, CONV_SLICE):
        hi = lo + CONV_SLICE
        xpad_ref[SUBLANES:SUBLANES + tm, lo:hi] = _dot(u, wbig_ref[:, lo:hi])
        nxt = next(others, None)
        if nxt is not None:
            nxt()
        conv = convw_ref[0:1, lo:hi] * xpad_ref[first:first + tm, lo:hi]
        for kk in range(1, CONV_K):
            conv = conv + convw_ref[kk:kk + 1, lo:hi] * xpad_ref[first + kk:first + kk + tm, lo:hi]
        act_ref[:, lo:hi] = _silu(conv)
    for nxt in others:
        nxt()
    tail_ref[...] = xpad_ref[tm:tm + SUBLANES, :]
    xpad_ref[0:SUBLANES, :] = xpad_ref[tm:tm + SUBLANES, :]


def _inproj(x2d, nw, wbig, wsm, convw, hist, *, tm, tiles_per_seq):
    rows = x2d.shape[0]
    ntiles = rows // tm
    row_spec = lambda n: pl.BlockSpec((tm, n), lambda i: (i, 0))
    const_spec = lambda a: pl.BlockSpec(a.shape, lambda i: (0, 0))
    body = functools.partial(_inproj_body, tm=tm, tiles_per_seq=tiles_per_seq)
    return pl.pallas_call(
        body,
        grid=(ntiles,),
        in_specs=[row_spec(D_MODEL), const_spec(nw), const_spec(wbig), const_spec(wsm), const_spec(convw),
                  const_spec(hist)],
        out_specs=[row_spec(GDN_QKV), row_spec(GDN_HD), row_spec(FOX_HD), row_spec(FOX_HD), row_spec(FOX_HD),
                   row_spec(LANES), pl.BlockSpec((SUBLANES, GDN_QKV), lambda i: (i, 0))],
        out_shape=[jax.ShapeDtypeStruct((rows, GDN_QKV), F32), jax.ShapeDtypeStruct((rows, GDN_HD), F32),
                   jax.ShapeDtypeStruct((rows, FOX_HD), BF16), jax.ShapeDtypeStruct((rows, FOX_HD), BF16),
                   jax.ShapeDtypeStruct((rows, FOX_HD), BF16), jax.ShapeDtypeStruct((rows, LANES), F32),
                   jax.ShapeDtypeStruct((ntiles * SUBLANES, GDN_QKV), F32)],
        scratch_shapes=[pltpu.VMEM((tm + SUBLANES, GDN_QKV), F32)],
        compiler_params=pltpu.CompilerParams(dimension_semantics=("arbitrary",), vmem_limit_bytes=VMEM_LIMIT),
        name="inproj",
    )(x2d, nw, wbig, wsm, convw, hist)


def _unit_lower_inverses(a_mats, r_i, c_i, size):
    n = range(len(a_mats))
    base = min(INV_BASE, size)
    eye = jnp.where(r_i == c_i, 1.0, 0.0).astype(F32)
    rc_xor = r_i ^ c_i
    p_pow = [jnp.where(rc_xor < base, -a_mats[i], 0.0) for i in n]
    t_inv = [eye + p_pow[i] for i in n]
    for _ in range(int(math.log2(base)) - 1):
        p16 = [p_pow[i].astype(BF16) for i in n]
        p_pow = [_dot(p16[i], p16[i]) for i in n]
        t_inv = [t_inv[i] + _dot(t_inv[i].astype(BF16), p_pow[i].astype(BF16)) for i in n]
    half = base
    while half < size:
        lower_left = (rc_xor >= half) & (rc_xor < 2 * half)
        t16 = [t_inv[i].astype(BF16) for i in n]
        y16 = [_dot(jnp.where(lower_left, a_mats[i], 0.0).astype(BF16), t16[i]).astype(BF16) for i in n]
        t_inv = [t_inv[i] - _dot(t16[i], y16[i]) for i in n]
        half *= 2
    return t_inv


def _gdn_body(act_ref, z_ref, sm_ref, s0_ref, avec_ref, dtvec_ref, normw_ref, o_ref, sout_ref, state_ref,
              *, rows, chunk):
    j = pl.program_id(1)

    @pl.when(j == 0)
    def _():
        state_ref[...] = s0_ref[...]

    sm = sm_ref[...]
    beta_all = _sigmoid(sm)
    g_all = avec_ref[...] * _softplus(sm + dtvec_ref[...])

    rr = lax.broadcasted_iota(jnp.int32, (rows, rows), 0)
    cc = lax.broadcasted_iota(jnp.int32, (rows, rows), 1)
    tri = jnp.where((rr >= cc) & ((rr ^ cc) < chunk), 1.0, 0.0).astype(BF16)
    g1, g2, g3 = _split3(g_all)
    gc_all = _dot(tri, g1) + _dot(tri, g2) + _dot(tri, g3)
    sel = jnp.where(lax.broadcasted_iota(jnp.int32, (SUBLANES, LANES), 1)
                    == lax.broadcasted_iota(jnp.int32, (SUBLANES, LANES), 0) + LANE_DECAY, 1.0, 0.0).astype(BF16)
    c1, c2, c3 = _split3(gc_all)
    gc_rows = _nt_dot(sel, c1) + _nt_dot(sel, c2) + _nt_dot(sel, c3)
    egc_all = jnp.exp(gc_all)
    normw = normw_ref[...]
    r_i = lax.broadcasted_iota(jnp.int32, (chunk, chunk), 0)
    c_i = lax.broadcasted_iota(jnp.int32, (chunk, chunk), 1)
    incl = r_i >= c_i
    strict = r_i > c_i

    heads = range(GDN_HEADS)
    pairs = [(c, h) for c in range(rows // chunk) for h in heads]
    gl_rows = [gc_all[(c + 1) * chunk - 1:(c + 1) * chunk, :] for c in range(rows // chunk)]
    ekg_all = [jnp.exp(gl_rows[c] - gc_all[c * chunk:(c + 1) * chunk, :]) for c in range(rows // chunk)]
    egl_all = [jnp.exp(gl_rows[c]) for c in range(rows // chunk)]
    qgs, kgs, rhss, a_mats, qks, egls = [], [], [], [], [], []
    for c, h in pairs:
        lo = c * chunk
        hi = lo + chunk
        q = act_ref[lo:hi, h * GDN_D:(h + 1) * GDN_D]
        k = act_ref[lo:hi, GDN_HD + h * GDN_D:GDN_HD + (h + 1) * GDN_D]
        v = act_ref[lo:hi, 2 * GDN_HD + h * GDN_D:2 * GDN_HD + (h + 1) * GDN_D]
        q = q * lax.rsqrt(jnp.sum(q * q, axis=-1, keepdims=True) + EPS) * (GDN_D ** -0.5)
        k = k * lax.rsqrt(jnp.sum(k * k, axis=-1, keepdims=True) + EPS)
        beta = beta_all[lo:hi, LANE_BETA + h:LANE_BETA + h + 1]
        gcol = gc_all[lo:hi, LANE_DECAY + h:LANE_DECAY + h + 1]
        egc = egc_all[lo:hi, LANE_DECAY + h:LANE_DECAY + h + 1]
        grow = gc_rows[h:h + 1, lo:hi]
        kb = k * beta
        k16 = k.astype(BF16)
        decay = jnp.where(incl, jnp.exp(jnp.where(incl, gcol - grow, 0.0)), 0.0)
        qgs.append(q * egc)
        kgs.append((k * ekg_all[c][:, LANE_DECAY + h:LANE_DECAY + h + 1]).astype(BF16))
        egls.append(egl_all[c][:, LANE_DECAY + h:LANE_DECAY + h + 1])
        rhss.append(jnp.concatenate([v * beta, kb * egc], axis=1).astype(BF16))
        a_mats.append(jnp.where(strict, _nt_dot(kb.astype(BF16), k16) * decay, 0.0))
        qks.append(jnp.where(incl, _nt_dot(q.astype(BF16), k16) * decay, 0.0).astype(BF16))
    t_invs = _unit_lower_inverses(a_mats, r_i, c_i, chunk)
    uws = [_dot(t_invs[i].astype(BF16), rhss[i]) for i in range(len(pairs))]

    state = [state_ref[h] for h in heads]
    for c in range(rows // chunk):
        idx = [c * GDN_HEADS + h for h in heads]
        wss = [_dot(jnp.concatenate([uws[i][:, GDN_D:], qgs[i]], axis=0).astype(BF16), state[h].astype(BF16))
               for h, i in zip(heads, idx)]
        v_news = [(uws[i][:, :GDN_D] - ws[:chunk]).astype(BF16) for i, ws in zip(idx, wss)]
        outs = [ws[chunk:] + _dot(qks[i], v_new) for i, ws, v_new in zip(idx, wss, v_news)]
        state = [state[h] * egls[i] + _tn_dot(kgs[i], v_new) for h, i, v_new in zip(heads, idx, v_news)]
        for h in heads:
            zg = _silu(z_ref[c * chunk:(c + 1) * chunk, h * GDN_D:(h + 1) * GDN_D])
            o = outs[h]
            o = o * lax.rsqrt(jnp.mean(o * o, axis=-1, keepdims=True) + EPS) * normw * zg
            o_ref[c * chunk:(c + 1) * chunk, h * GDN_D:(h + 1) * GDN_D] = o.astype(o_ref.dtype)
    for h in heads:
        state_ref[h] = state[h]

    @pl.when(j == pl.num_programs(1) - 1)
    def _():
        sout_ref[0] = state_ref[...]


def _gdn(act, z, sm, s0, avec, dtvec, normw, *, batch, seq, rows, chunk):
    nblk = seq // rows
    row_spec = lambda n: pl.BlockSpec((rows, n), lambda b, j: (b * nblk + j, 0))
    const2 = lambda a: pl.BlockSpec(a.shape, lambda b, j: (0, 0))
    body = functools.partial(_gdn_body, rows=rows, chunk=chunk)
    return pl.pallas_call(
        body,
        grid=(batch, nblk),
        in_specs=[row_spec(GDN_QKV), row_spec(GDN_HD), row_spec(LANES),
                  pl.BlockSpec(s0.shape, lambda b, j: (0, 0, 0)), const2(avec), const2(dtvec), const2(normw)],
        out_specs=[row_spec(GDN_HD), pl.BlockSpec((1, GDN_HEADS, GDN_D, GDN_D), lambda b, j: (b, 0, 0, 0))],
        out_shape=[jax.ShapeDtypeStruct((batch * seq, GDN_HD), BF16),
                   jax.ShapeDtypeStruct((batch, GDN_HEADS, GDN_D, GDN_D), F32)],
        scratch_shapes=[pltpu.VMEM((GDN_HEADS, GDN_D, GDN_D), F32)],
        compiler_params=pltpu.CompilerParams(dimension_semantics=("arbitrary", "arbitrary"),
                                             vmem_limit_bytes=VMEM_LIMIT),
        name="gdn",
    )(act, z, sm, s0, avec, dtvec, normw)


def _bias_constants():
    sel = np.zeros((LANES, 2 * FOX_HD), np.float32)
    ones = np.zeros((1, 2 * FOX_HD), np.float32)
    for head in range(FOX_HEADS):
        base = (head // 2) * LANES + BIAS_STRIDE * (head % 2)
        for t in range(3):
            src = LANE_FORGET + t * FOX_HEADS + head
            sel[src, base + t] = 1.0
            sel[src, FOX_HD + base + 3 + t] = -1.0
            ones[0, base + 3 + t] = 1.0
            ones[0, FOX_HD + base + t] = 1.0
    return jnp.asarray(sel, BF16), jnp.asarray(ones)


def _pack3(x):
    hi = x.astype(BF16).astype(F32)
    rest = x - hi
    mid = rest.astype(BF16).astype(F32)
    lo = (rest - mid).astype(BF16).astype(F32)
    return (hi + pltpu.roll(mid, FOX_HEADS, 1) + pltpu.roll(lo, 2 * FOX_HEADS, 1)).astype(BF16)


def _cumgate_body(sm_ref, fb_ref, c0_ref, sel_ref, ones_ref, ccol_ref, qb_ref, kb_ref, *, seq, blk):
    lane = lax.broadcasted_iota(jnp.int32, (1, LANES), 1)
    valid = (lane >= LANE_FORGET) & (lane < LANE_FORGET + FOX_HEADS)
    r_i = lax.broadcasted_iota(jnp.int32, (blk, blk), 0)
    c_i = lax.broadcasted_iota(jnp.int32, (blk, blk), 1)
    tri = jnp.where(r_i >= c_i, 1.0, 0.0).astype(BF16)
    carry = c0_ref[...]
    for i in range(seq // blk):
        xg = sm_ref[i * blk:(i + 1) * blk, :] + fb_ref[...]
        logf = jnp.where(valid, jnp.minimum(xg, 0.0) - jnp.log1p(jnp.exp(-jnp.abs(xg))), 0.0)
        cs3 = _dot(tri, _pack3(logf))
        cs = cs3 + pltpu.roll(cs3, LANES - FOX_HEADS, 1) + pltpu.roll(cs3, LANES - 2 * FOX_HEADS, 1)
        cs = jnp.where(valid, carry + cs, 0.0)
        ccol_ref[i * blk:(i + 1) * blk, :] = cs
        bias = _dot(_pack3(cs * LOG2E), sel_ref[...]) + ones_ref[...]
        qb_ref[i * blk:(i + 1) * blk, :] = bias[:, :FOX_HD].astype(BF16)
        kb_ref[i * blk:(i + 1) * blk, :] = bias[:, FOX_HD:].astype(BF16)
        carry = cs[blk - 1:blk, :]


def _cumgate(sm, fb, c0, consts, *, batch, seq):
    blk = min(CUMGATE_ROWS, seq)
    sel, ones = consts
    body = functools.partial(_cumgate_body, seq=seq, blk=blk)
    c2 = lambda a: pl.BlockSpec(a.shape, lambda b: (0, 0))
    return pl.pallas_call(
        body,
        grid=(batch,),
        in_specs=[pl.BlockSpec((seq, LANES), lambda b: (b, 0)), c2(fb), c2(c0), c2(sel), c2(ones)],
        out_specs=[pl.BlockSpec((seq, LANES), lambda b: (b, 0)), pl.BlockSpec((seq, FOX_HD), lambda b: (b, 0)),
                   pl.BlockSpec((seq, FOX_HD), lambda b: (b, 0))],
        out_shape=[jax.ShapeDtypeStruct((batch * seq, LANES), F32),
                   jax.ShapeDtypeStruct((batch * seq, FOX_HD), BF16),
                   jax.ShapeDtypeStruct((batch * seq, FOX_HD), BF16)],
        compiler_params=pltpu.CompilerParams(dimension_semantics=("arbitrary",), vmem_limit_bytes=VMEM_LIMIT),
        name="cumgate",
    )(sm, fb, c0, sel, ones)


def _fox_body(q_ref, qb_ref, k_ref, kb_ref, v_ref, km_ref, kbm_ref, vm_ref, o_ref,
              s_scr, p_scr, m_scr, alpha_scr, acc_scr, *, tq):
    i = pl.program_id(2)
    wide = tq
    lane = lax.broadcasted_iota(jnp.int32, (1, LANES), 1)
    q_all = q_ref[...]
    qb_all = qb_ref[...]
    zero = jnp.zeros_like(q_all)
    one = jnp.ones_like(q_all[0:1, :])
    own = [lane < FOX_DH, lane >= FOX_DH]
    q_aug = []
    for hh in range(2):
        bias_lanes = (lane >= BIAS_STRIDE * hh) & (lane < BIAS_STRIDE * (hh + 1))
        q_aug.append(jnp.concatenate([jnp.where(own[hh], q_all, zero), jnp.where(bias_lanes, qb_all, zero)], axis=1))

    def keys(start, width):
        return jnp.concatenate([k_ref[pl.ds(start, width), :], kb_ref[pl.ds(start, width), :]], axis=1)

    def values(start, width, hh):
        return jnp.where(own[hh], v_ref[pl.ds(start, width), :], one)

    n_full = i
    two = range(2)

    diag_start = pl.multiple_of(n_full * wide, wide)
    meta_mask = lax.broadcasted_iota(jnp.int32, (tq, LANES), 1) < N_META
    causal = lax.broadcasted_iota(jnp.int32, (tq, wide), 1) <= lax.broadcasted_iota(jnp.int32, (tq, wide), 0)
    km_aug = jnp.concatenate([km_ref[...], kbm_ref[...]], axis=1)
    kd_aug = keys(diag_start, wide)
    s_meta = [jnp.where(meta_mask, _nt_dot(q_aug[hh], km_aug), MASK_VALUE) for hh in two]
    s_diag = [jnp.where(causal, _nt_dot(q_aug[hh], kd_aug), MASK_VALUE) for hh in two]

    n_pairs = (n_full + 1) // 2

    def block_start(t):
        return pl.multiple_of(jnp.clip(t, 0, jnp.maximum(n_full - 1, 0)) * wide, wide)

    def value_start(t):
        return pl.multiple_of(jnp.where(t < 0, diag_start, block_start(t)), wide)

    def pipeline_step(u, carry):
        one_step(2 * u, 0, None)
        one_step(2 * u + 1, 1, 2 * u + 1 < n_full)
        return carry

    def one_step(t, cur, real_block):
        nxt = 1 - cur
        prev_start = value_start(t - 1)
        next_aug = keys(block_start(t + 1), wide)
        for hh in two:
            acc_scr[hh] = alpha_scr[hh] * acc_scr[hh] + _dot(p_scr[nxt, hh], values(prev_start, wide, hh))
        for hh in two:
            m_old = m_scr[hh]
            m_new = jnp.maximum(m_old, jnp.max(s_scr[cur, hh], axis=-1, keepdims=True))
            if real_block is not None:
                m_new = jnp.where(real_block, m_new, m_old)
            alpha_scr[hh] = jnp.exp2(m_old - m_new)
            m_scr[hh] = m_new
            p_scr[cur, hh] = jnp.exp2(s_scr[cur, hh] - m_new).astype(BF16)
        for hh in two:
            s_scr[nxt, hh] = _nt_dot(q_aug[hh], next_aug)

    first_aug = keys(block_start(0), wide)
    for hh in two:
        s_scr[0, hh] = _nt_dot(q_aug[hh], first_aug)
    for hh in two:
        m_first = jnp.maximum(jnp.max(s_meta[hh], axis=-1, keepdims=True),
                              jnp.max(s_diag[hh], axis=-1, keepdims=True))
        m_scr[hh] = m_first
        alpha_scr[hh] = jnp.ones((tq, 1), F32)
        p_scr[1, hh] = jnp.exp2(s_diag[hh] - m_first).astype(BF16)
        acc_scr[hh] = _dot(jnp.exp2(s_meta[hh] - m_first).astype(BF16), jnp.where(own[hh], vm_ref[...], one))
    lax.fori_loop(0, n_pairs, pipeline_step, 0)

    t_last = 2 * n_pairs - 1
    last_start = value_start(t_last)
    keep_last = jnp.where(t_last < n_full, 1.0, 0.0)
    outs = []
    for hh in two:
        acc = alpha_scr[hh] * acc_scr[hh] + keep_last * _dot(p_scr[1, hh], values(last_start, wide, hh))
        outs.append(acc / pltpu.roll(acc, FOX_DH, 1))
    o_ref[...] = jnp.where(own[0], outs[0], outs[1]).astype(o_ref.dtype)


def _fox(fq, qb, fk, kb, fv, km, kbm, vm, *, batch, seq, tq):
    nq = seq // tq
    npair = FOX_HEADS // 2
    body = functools.partial(_fox_body, tq=tq)
    qspec = pl.BlockSpec((tq, LANES), lambda b, p, i: (b * nq + i, p))
    kspec = pl.BlockSpec((seq, LANES), lambda b, p, i: (b, p))
    mspec = pl.BlockSpec((LANES, LANES), lambda b, p, i: (0, p))
    return pl.pallas_call(
        body,
        grid=(batch, npair, nq),
        in_specs=[qspec, qspec, kspec, kspec, kspec, mspec, mspec, mspec],
        out_specs=qspec,
        out_shape=jax.ShapeDtypeStruct((batch * seq, FOX_HD), BF16),
        scratch_shapes=[pltpu.VMEM((2, 2, tq, tq), F32), pltpu.VMEM((2, 2, tq, tq), BF16),
                        pltpu.VMEM((2, tq, 1), F32), pltpu.VMEM((2, tq, 1), F32), pltpu.VMEM((2, tq, LANES), F32)],
        compiler_params=pltpu.CompilerParams(dimension_semantics=("arbitrary", "arbitrary", "arbitrary"),
                                             vmem_limit_bytes=VMEM_LIMIT),
        name="fox",
    )(fq, qb, fk, kb, fv, km, kbm, vm)


def _ffn_body(x_ref, og_ref, of_ref, wo_ref, nw_ref, wg_ref, wu_ref, wd_ref, fw_ref, out_ref, act_ref, *, fchunk):
    h1 = x_ref[...] + _dot(og_ref[...], wo_ref[0:GDN_HD, :]) + _dot(of_ref[...], wo_ref[GDN_HD:GDN_HD + FOX_HD, :])
    n = (h1 * lax.rsqrt(jnp.mean(h1 * h1, axis=-1, keepdims=True) + EPS) * nw_ref[...]).astype(BF16)
    d_ff = wg_ref.shape[1]
    for c in range(d_ff // fchunk):
        sl = slice(c * fchunk, (c + 1) * fchunk)
        g = _dot(n, wg_ref[:, sl])
        up = _dot(n, wu_ref[:, sl])
        act_ref[:, sl] = (_silu(g) * up).astype(BF16)
    acc = h1 + _dot(act_ref[...], wd_ref[...])
    out_ref[...] = acc * lax.rsqrt(jnp.mean(acc * acc, axis=-1, keepdims=True) + EPS) * fw_ref[...]


def _ffn(x2d, og, of, wo, nw, wg, wu, wd, fw, tm):
    rows = x2d.shape[0]
    row_spec = lambda n: pl.BlockSpec((tm, n), lambda i: (i, 0))
    const_spec = lambda a: pl.BlockSpec(a.shape, lambda i: (0, 0), pipeline_mode=pl.Buffered(1))
    assert wg.shape[1] % FFN_COLS == 0
    body = functools.partial(_ffn_body, fchunk=FFN_COLS)
    return pl.pallas_call(
        body,
        grid=(rows // tm,),
        in_specs=[row_spec(D_MODEL), row_spec(GDN_HD), row_spec(FOX_HD), const_spec(wo), const_spec(nw),
                  const_spec(wg), const_spec(wu), const_spec(wd), const_spec(fw)],
        out_specs=row_spec(D_MODEL),
        out_shape=jax.ShapeDtypeStruct((rows, D_MODEL), F32),
        scratch_shapes=[pltpu.VMEM((tm, wg.shape[1]), BF16)],
        compiler_params=pltpu.CompilerParams(dimension_semantics=("arbitrary",), vmem_limit_bytes=VMEM_LIMIT),
        name="ffn",
    )(x2d, og, of, wo, nw, wg, wu, wd, fw)


def _lane_vec(vals, offset):
    return jnp.zeros((1, LANES), F32).at[0, offset:offset + vals.shape[0]].set(vals.astype(F32))


def kernel(x, meta_tokens, attn_norm_w, w_in, conv_w, a_log, dt_bias, gdn_norm_w, fgate_b, w_out, ffn_norm_w,
           w_gate, w_up, w_down, final_norm_w):
    batch, seq, _ = x.shape
    assert w_in.shape[0] == 1, "one layer"
    for tile_rows in (INPROJ_ROWS, GDN_ROWS, CUMGATE_ROWS, FOX_ROWS, FFN_ROWS):
        assert seq % tile_rows == 0
    w = w_in[0]
    gate_start = GDN_QKV + GDN_HD
    fox_start = gate_start + 2 * GDN_HEADS
    forget_start = fox_start + 3 * FOX_HD
    assert w.shape == (D_MODEL, forget_start + FOX_HEADS)
    wbig = jnp.concatenate([w[:, :gate_start], w[:, fox_start:forget_start]], axis=1).astype(BF16)
    wsm = jnp.concatenate([w[:, gate_start:fox_start], w[:, forget_start:],
                           jnp.zeros((D_MODEL, LANES - 2 * GDN_HEADS - FOX_HEADS), F32)], axis=1).astype(BF16)
    anw = attn_norm_w[0][None, :]
    avec = _lane_vec(-jnp.exp(a_log[0]), LANE_DECAY)
    dtvec = _lane_vec(dt_bias[0], LANE_DECAY)
    fbvec = _lane_vec(fgate_b[0], LANE_FORGET)
    gnw = gdn_norm_w[0][None, :]
    convw = conv_w[0]
    bias_consts = _bias_constants()

    x2d = x.reshape(batch * seq, D_MODEL)

    zeros_hist = jnp.zeros((SUBLANES, GDN_QKV), F32)
    act_m, z_m, _, fk_m, fv_m, sm_m, tail_m = _inproj(meta_tokens, anw, wbig, wsm, convw, zeros_hist,
                                                      tm=N_META, tiles_per_seq=1)
    zeros_state = jnp.zeros((GDN_HEADS, GDN_D, GDN_D), F32)
    _, s_meta = _gdn(act_m, z_m, sm_m, zeros_state, avec, dtvec, gnw,
                     batch=1, seq=N_META, rows=N_META, chunk=N_META)
    ccol_m, _, kb_m = _cumgate(sm_m, fbvec, jnp.zeros((1, LANES), F32), bias_consts, batch=1, seq=N_META)
    pad_rows = ((0, LANES - N_META), (0, 0))
    km = jnp.pad(fk_m, pad_rows)
    kbm = jnp.pad(kb_m, pad_rows)
    vm = jnp.pad(fv_m, pad_rows)

    act, z, fq, fk, fv, sm, _ = _inproj(x2d, anw, wbig, wsm, convw, tail_m,
                                        tm=INPROJ_ROWS, tiles_per_seq=seq // INPROJ_ROWS)
    o_gdn, _ = _gdn(act, z, sm, s_meta[0], avec, dtvec, gnw, batch=batch, seq=seq, rows=GDN_ROWS, chunk=GDN_CHUNK)
    _, qb, kb = _cumgate(sm, fbvec, ccol_m[N_META - 1:], bias_consts, batch=batch, seq=seq)
    o_fox = _fox(fq, qb, fk, kb, fv, km, kbm, vm, batch=batch, seq=seq, tq=FOX_ROWS)

    out = _ffn(x2d, o_gdn, o_fox, w_out[0].astype(BF16), ffn_norm_w[0][None, :], w_gate[0].astype(BF16),
               w_up[0].astype(BF16), w_down[0].astype(BF16), final_norm_w[None, :], tm=FFN_ROWS)
    return out.reshape(batch, seq, D_MODEL)
```

```python
import functools
import math

import numpy as np
import jax
import jax.numpy as jnp
from jax import lax
from jax.experimental import pallas as pl
from jax.experimental.pallas import tpu as pltpu

F32 = jnp.float32
BF16 = jnp.bfloat16

D_MODEL = 1024
N_META = 16
GDN_HEADS = 4
GDN_D = 128
FOX_HEADS = 8
FOX_DH = 64
CONV_K = 4
EPS = 1e-6
MASK_VALUE = -1e30
LOG2E = 1.4426950408889634
GDN_QKV = 3 * GDN_HEADS * GDN_D
GDN_HD = GDN_HEADS * GDN_D
FOX_HD = FOX_HEADS * FOX_DH
LANES = 128
SUBLANES = 8
LANE_BETA = 0
LANE_DECAY = GDN_HEADS
LANE_FORGET = 2 * GDN_HEADS
BIAS_STRIDE = 8
INV_BASE = 16
CONV_SLICE = 512
VMEM_LIMIT = 56 * 1024 * 1024
INPROJ_ROWS = 1024
GDN_ROWS = 512
GDN_CHUNK = 128
CUMGATE_ROWS = 256
FOX_ROWS = 512
FFN_ROWS = 1024
FFN_COLS = 256


def _nt_dot(a, b):
    return lax.dot_general(a, b, (((1,), (1,)), ((), ())), preferred_element_type=F32)


def _tn_dot(a, b):
    return lax.dot_general(a, b, (((0,), (0,)), ((), ())), preferred_element_type=F32)


def _dot(a, b):
    return jnp.dot(a, b, preferred_element_type=F32)


def _split3(x):
    x1 = x.astype(BF16)
    r1 = x - x1.astype(F32)
    x2 = r1.astype(BF16)
    x3 = (r1 - x2.astype(F32)).astype(BF16)
    return x1, x2, x3


def _softplus(x):
    return jnp.maximum(x, 0.0) + jnp.log1p(jnp.exp(-jnp.abs(x)))


def _sigmoid(x):
    return 1.0 / (1.0 + jnp.exp(-x))


def _silu(x):
    return x * _sigmoid(x)


def _inproj_body(x_ref, nw_ref, wbig_ref, wsm_ref, convw_ref, hist_ref,
                 act_ref, z_ref, fq_ref, fk_ref, fv_ref, sm_ref, tail_ref, xpad_ref, *, tm, tiles_per_seq):
    @pl.when(pl.program_id(0) % tiles_per_seq == 0)
    def _():
        xpad_ref[0:SUBLANES, :] = hist_ref[...]

    x = x_ref[...]
    u = (x * lax.rsqrt(jnp.mean(x * x, axis=-1, keepdims=True) + EPS) * nw_ref[...]).astype(BF16)
    o0 = GDN_QKV
    o1 = o0 + GDN_HD
    o2 = o1 + FOX_HD
    o3 = o2 + FOX_HD
    o4 = o3 + FOX_HD

    def proj_z():
        z_ref[...] = _dot(u, wbig_ref[:, o0:o1])

    def proj_fq():
        fq_ref[...] = (_dot(u, wbig_ref[:, o1:o2]) * (LOG2E * FOX_DH ** -0.5)).astype(BF16)

    def proj_fk():
        fk_ref[...] = _dot(u, wbig_ref[:, o2:o3]).astype(BF16)

    def proj_fv():
        fv_ref[...] = _dot(u, wbig_ref[:, o3:o4]).astype(BF16)

    def proj_sm():
        sm_ref[...] = _dot(u, wsm_ref[...])

    others = iter([proj_z, proj_fq, proj_fk, proj_fv, proj_sm])
    first = SUBLANES - (CONV_K - 1)
    for lo in range(0, GDN_QKV, CONV_SLICE):
        hi = lo + CONV_SLICE
        xpad_ref[SUBLANES:SUBLANES + tm, lo:hi] = _dot(u, wbig_ref[:, lo:hi])
        nxt = next(others, None)
        if nxt is not None:
            nxt()
        conv = convw_ref[0:1, lo:hi] * xpad_ref[first:first + tm, lo:hi]
        for kk in range(1, CONV_K):
            conv = conv + convw_ref[kk:kk + 1, lo:hi] * xpad_ref[first + kk:first + kk + tm, lo:hi]
        act_ref[:, lo:hi] = _silu(conv)
    for nxt in others:
        nxt()
    tail_ref[...] = xpad_ref[tm:tm + SUBLANES, :]
    xpad_ref[0:SUBLANES, :] = xpad_ref[tm:tm + SUBLANES, :]


def _inproj(x2d, nw, wbig, wsm, convw, hist, *, tm, tiles_per_seq):
    rows = x2d.shape[0]
    ntiles = rows // tm
    row_spec = lambda n: pl.BlockSpec((tm, n), lambda i: (i, 0))
    const_spec = lambda a: pl.BlockSpec(a.shape, lambda i: (0, 0))
    body = functools.partial(_inproj_body, tm=tm, tiles_per_seq=tiles_per_seq)
    return pl.pallas_call(
        body,
        grid=(ntiles,),
        in_specs=[row_spec(D_MODEL), const_spec(nw), const_spec(wbig), const_spec(wsm), const_spec(convw),
                  const_spec(hist)],
        out_specs=[row_spec(GDN_QKV), row_spec(GDN_HD), row_spec(FOX_HD), row_spec(FOX_HD), row_spec(FOX_HD),
                   row_spec(LANES), pl.BlockSpec((SUBLANES, GDN_QKV), lambda i: (i, 0))],
        out_shape=[jax.ShapeDtypeStruct((rows, GDN_QKV), F32), jax.ShapeDtypeStruct((rows, GDN_HD), F32),
                   jax.ShapeDtypeStruct((rows, FOX_HD), BF16), jax.ShapeDtypeStruct((rows, FOX_HD), BF16),
                   jax.ShapeDtypeStruct((rows, FOX_HD), BF16), jax.ShapeDtypeStruct((rows, LANES), F32),
                   jax.ShapeDtypeStruct((ntiles * SUBLANES, GDN_QKV), F32)],
        scratch_shapes=[pltpu.VMEM((tm + SUBLANES, GDN_QKV), F32)],
        compiler_params=pltpu.CompilerParams(dimension_semantics=("arbitrary",), vmem_limit_bytes=VMEM_LIMIT),
        name="inproj",
    )(x2d, nw, wbig, wsm, convw, hist)


def _unit_lower_inverses(a_mats, r_i, c_i, size):
    n = range(len(a_mats))
    base = min(INV_BASE, size)
    eye = jnp.where(r_i == c_i, 1.0, 0.0).astype(F32)
    rc_xor = r_i ^ c_i
    p_pow = [jnp.where(rc_xor < base, -a_mats[i], 0.0) for i in n]
    t_inv = [eye + p_pow[i] for i in n]
    for _ in range(int(math.log2(base)) - 1):
        p16 = [p_pow[i].astype(BF16) for i in n]
        p_pow = [_dot(p16[i], p16[i]) for i in n]
        t_inv = [t_inv[i] + _dot(t_inv[i].astype(BF16), p_pow[i].astype(BF16)) for i in n]
    half = base
    while half < size:
        lower_left = (rc_xor >= half) & (rc_xor < 2 * half)
        t16 = [t_inv[i].astype(BF16) for i in n]
        y16 = [_dot(jnp.where(lower_left, a_mats[i], 0.0).astype(BF16), t16[i]).astype(BF16) for i in n]
        t_inv = [t_inv[i] - _dot(t16[i], y16[i]) for i in n]
        half *= 2
    return t_inv


def _gdn_body(act_ref, z_ref, sm_ref, s0_ref, avec_ref, dtvec_ref, normw_ref, o_ref, sout_ref, state_ref,
              *, rows, chunk):
    j = pl.program_id(1)

    @pl.when(j == 0)
    def _():
        state_ref[...] = s0_ref[...]

    sm = sm_ref[...]
    beta_all = _sigmoid(sm)
    g_all = avec_ref[...] * _softplus(sm + dtvec_ref[...])

    rr = lax.broadcasted_iota(jnp.int32, (rows, rows), 0)
    cc = lax.broadcasted_iota(jnp.int32, (rows, rows), 1)
    tri = jnp.where((rr >= cc) & ((rr ^ cc) < chunk), 1.0, 0.0).astype(BF16)
    g1, g2, g3 = _split3(g_all)
    gc_all = _dot(tri, g1) + _dot(tri, g2) + _dot(tri, g3)
    sel = jnp.where(lax.broadcasted_iota(jnp.int32, (SUBLANES, LANES), 1)
                    == lax.broadcasted_iota(jnp.int32, (SUBLANES, LANES), 0) + LANE_DECAY, 1.0, 0.0).astype(BF16)
    c1, c2, c3 = _split3(gc_all)
    gc_rows = _nt_dot(sel, c1) + _nt_dot(sel, c2) + _nt_dot(sel, c3)
    egc_all = jnp.exp(gc_all)
    normw = normw_ref[...]
    r_i = lax.broadcasted_iota(jnp.int32, (chunk, chunk), 0)
    c_i = lax.broadcasted_iota(jnp.int32, (chunk, chunk), 1)
    incl = r_i >= c_i
    strict = r_i > c_i

    heads = range(GDN_HEADS)
    pairs = [(c, h) for c in range(rows // chunk) for h in heads]
    gl_rows = [gc_all[(c + 1) * chunk - 1:(c + 1) * chunk, :] for c in range(rows // chunk)]
    ekg_all = [jnp.exp(gl_rows[c] - gc_all[c * chunk:(c + 1) * chunk, :]) for c in range(rows // chunk)]
    egl_all = [jnp.exp(gl_rows[c]) for c in range(rows // chunk)]
    qgs, kgs, rhss, a_mats, qks, egls = [], [], [], [], [], []
    for c, h in pairs:
        lo = c * chunk
        hi = lo + chunk
        q = act_ref[lo:hi, h * GDN_D:(h + 1) * GDN_D]
        k = act_ref[lo:hi, GDN_HD + h * GDN_D:GDN_HD + (h + 1) * GDN_D]
        v = act_ref[lo:hi, 2 * GDN_HD + h * GDN_D:2 * GDN_HD + (h + 1) * GDN_D]
        q = q * lax.rsqrt(jnp.sum(q * q, axis=-1, keepdims=True) + EPS) * (GDN_D ** -0.5)
        k = k * lax.rsqrt(jnp.sum(k * k, axis=-1, keepdims=True) + EPS)
        beta = beta_all[lo:hi, LANE_BETA + h:LANE_BETA + h + 1]
        gcol = gc_all[lo:hi, LANE_DECAY + h:LANE_DECAY + h + 1]
        egc = egc_all[lo:hi, LANE_DECAY + h:LANE_DECAY + h + 1]
        grow = gc_rows[h:h + 1, lo:hi]
        kb = k * beta
        k16 = k.astype(BF16)
        decay = jnp.where(incl, jnp.exp(jnp.where(incl, gcol - grow, 0.0)), 0.0)
        qgs.append(q * egc)
        kgs.append((k * ekg_all[c][:, LANE_DECAY + h:LANE_DECAY + h + 1]).astype(BF16))
        egls.append(egl_all[c][:, LANE_DECAY + h:LANE_DECAY + h + 1])
        rhss.append(jnp.concatenate([v * beta, kb * egc], axis=1).astype(BF16))
        a_mats.append(jnp.where(strict, _nt_dot(kb.astype(BF16), k16) * decay, 0.0))
        qks.append(jnp.where(incl, _nt_dot(q.astype(BF16), k16) * decay, 0.0).astype(BF16))
    t_invs = _unit_lower_inverses(a_mats, r_i, c_i, chunk)
    uws = [_dot(t_invs[i].astype(BF16), rhss[i]) for i in range(len(pairs))]

    state = [state_ref[h] for h in heads]
    for c in range(rows // chunk):
        idx = [c * GDN_HEADS + h for h in heads]
        wss = [_dot(jnp.concatenate([uws[i][:, GDN_D:], qgs[i]], axis=0).astype(BF16), state[h].astype(BF16))
               for h, i in zip(heads, idx)]
        v_news = [(uws[i][:, :GDN_D] - ws[:chunk]).astype(BF16) for i, ws in zip(idx, wss)]
        outs = [ws[chunk:] + _dot(qks[i], v_new) for i, ws, v_new in zip(idx, wss, v_news)]
        state = [state[h] * egls[i] + _tn_dot(kgs[i], v_new) for h, i, v_new in zip(heads, idx, v_news)]
        for h in heads:
            zg = _silu(z_ref[c * chunk:(c + 1) * chunk, h * GDN_D:(h + 1) * GDN_D])
            o = outs[h]
            o = o * lax.rsqrt(jnp.mean(o * o, axis=-1, keepdims=True) + EPS) * normw * zg
            o_ref[c * chunk:(c + 1) * chunk, h * GDN_D:(h + 1) * GDN_D] = o.astype(o_ref.dtype)
    for h in heads:
        state_ref[h] = state[h]

    @pl.when(j == pl.num_programs(1) - 1)
    def _():
        sout_ref[0] = state_ref[...]


def _gdn(act, z, sm, s0, avec, dtvec, normw, *, batch, seq, rows, chunk):
    nblk = seq // rows
    row_spec = lambda n: pl.BlockSpec((rows, n), lambda b, j: (b * nblk + j, 0))
    const2 = lambda a: pl.BlockSpec(a.shape, lambda b, j: (0, 0))
    body = functools.partial(_gdn_body, rows=rows, chunk=chunk)
    return pl.pallas_call(
        body,
        grid=(batch, nblk),
        in_specs=[row_spec(GDN_QKV), row_spec(GDN_HD), row_spec(LANES),
                  pl.BlockSpec(s0.shape, lambda b, j: (0, 0, 0)), const2(avec), const2(dtvec), const2(normw)],
        out_specs=[row_spec(GDN_HD), pl.BlockSpec((1, GDN_HEADS, GDN_D, GDN_D), lambda b, j: (b, 0, 0, 0))],
        out_shape=[jax.ShapeDtypeStruct((batch * seq, GDN_HD), BF16),
                   jax.ShapeDtypeStruct((batch, GDN_HEADS, GDN_D, GDN_D), F32)],
        scratch_shapes=[pltpu.VMEM((GDN_HEADS, GDN_D, GDN_D), F32)],
        compiler_params=pltpu.CompilerParams(dimension_semantics=("arbitrary", "arbitrary"),
                                             vmem_limit_bytes=VMEM_LIMIT),
        name="gdn",
    )(act, z, sm, s0, avec, dtvec, normw)


def _bias_constants():
    sel = np.zeros((LANES, 2 * FOX_HD), np.float32)
    ones = np.zeros((1, 2 * FOX_HD), np.float32)
    for head in range(FOX_HEADS):
        base = (head // 2) * LANES + BIAS_STRIDE * (head % 2)
        for t in range(3):
            src = LANE_FORGET + t * FOX_HEADS + head
            sel[src, base + t] = 1.0
            sel[src, FOX_HD + base + 3 + t] = -1.0
            ones[0, base + 3 + t] = 1.0
            ones[0, FOX_HD + base + t] = 1.0
    return jnp.asarray(sel, BF16), jnp.asarray(ones)


def _pack3(x):
    hi = x.astype(BF16).astype(F32)
    rest = x - hi
    mid = rest.astype(BF16).astype(F32)
    lo = (rest - mid).astype(BF16).astype(F32)
    return (hi + pltpu.roll(mid, FOX_HEADS, 1) + pltpu.roll(lo, 2 * FOX_HEADS, 1)).astype(BF16)


def _cumgate_body(sm_ref, fb_ref, c0_ref, sel_ref, ones_ref, ccol_ref, qb_ref, kb_ref, *, seq, blk):
    lane = lax.broadcasted_iota(jnp.int32, (1, LANES), 1)
    valid = (lane >= LANE_FORGET) & (lane < LANE_FORGET + FOX_HEADS)
    r_i = lax.broadcasted_iota(jnp.int32, (blk, blk), 0)
    c_i = lax.broadcasted_iota(jnp.int32, (blk, blk), 1)
    tri = jnp.where(r_i >= c_i, 1.0, 0.0).astype(BF16)
    carry = c0_ref[...]
    for i in range(seq // blk):
        xg = sm_ref[i * blk:(i + 1) * blk, :] + fb_ref[...]
        logf = jnp.where(valid, jnp.minimum(xg, 0.0) - jnp.log1p(jnp.exp(-jnp.abs(xg))), 0.0)
        cs3 = _dot(tri, _pack3(logf))
        cs = cs3 + pltpu.roll(cs3, LANES - FOX_HEADS, 1) + pltpu.roll(cs3, LANES - 2 * FOX_HEADS, 1)
        cs = jnp.where(valid, carry + cs, 0.0)
        ccol_ref[i * blk:(i + 1) * blk, :] = cs
        bias = _dot(_pack3(cs * LOG2E), sel_ref[...]) + ones_ref[...]
        qb_ref[i * blk:(i + 1) * blk, :] = bias[:, :FOX_HD].astype(BF16)
        kb_ref[i * blk:(i + 1) * blk, :] = bias[:, FOX_HD:].astype(BF16)
        carry = cs[blk - 1:blk, :]


def _cumgate(sm, fb, c0, consts, *, batch, seq):
    blk = min(CUMGATE_ROWS, seq)
    sel, ones = consts
    body = functools.partial(_cumgate_body, seq=seq, blk=blk)
    c2 = lambda a: pl.BlockSpec(a.shape, lambda b: (0, 0))
    return pl.pallas_call(
        body,
        grid=(batch,),
        in_specs=[pl.BlockSpec((seq, LANES), lambda b: (b, 0)), c2(fb), c2(c0), c2(sel), c2(ones)],
        out_specs=[pl.BlockSpec((seq, LANES), lambda b: (b, 0)), pl.BlockSpec((seq, FOX_HD), lambda b: (b, 0)),
                   pl.BlockSpec((seq, FOX_HD), lambda b: (b, 0))],
        out_shape=[jax.ShapeDtypeStruct((batch * seq, LANES), F32),
                   jax.ShapeDtypeStruct((batch * seq, FOX_HD), BF16),
                   jax.ShapeDtypeStruct((batch * seq, FOX_HD), BF16)],
        compiler_params=pltpu.CompilerParams(dimension_semantics=("arbitrary",), vmem_limit_bytes=VMEM_LIMIT),
        name="cumgate",
    )(sm, fb, c0, sel, ones)


def _fox_body(q_ref, qb_ref, k_ref, kb_ref, v_ref, km_ref, kbm_ref, vm_ref, o_ref,
              s_scr, p_scr, m_scr, alpha_scr, acc_scr, *, tq):
    i = pl.program_id(2)
    wide = tq
    lane = lax.broadcasted_iota(jnp.int32, (1, LANES), 1)
    q_all = q_ref[...]
    qb_all = qb_ref[...]
    zero = jnp.zeros_like(q_all)
    one = jnp.ones_like(q_all[0:1, :])
    own = [lane < FOX_DH, lane >= FOX_DH]
    q_aug = []
    for hh in range(2):
        bias_lanes = (lane >= BIAS_STRIDE * hh) & (lane < BIAS_STRIDE * (hh + 1))
        q_aug.append(jnp.concatenate([jnp.where(own[hh], q_all, zero), jnp.where(bias_lanes, qb_all, zero)], axis=1))

    def keys(start, width):
        return jnp.concatenate([k_ref[pl.ds(start, width), :], kb_ref[pl.ds(start, width), :]], axis=1)

    def values(start, width, hh):
        return jnp.where(own[hh], v_ref[pl.ds(start, width), :], one)

    n_full = i
    two = range(2)

    diag_start = pl.multiple_of(n_full * wide, wide)
    meta_mask = lax.broadcasted_iota(jnp.int32, (tq, LANES), 1) < N_META
    causal = lax.broadcasted_iota(jnp.int32, (tq, wide), 1) <= lax.broadcasted_iota(jnp.int32, (tq, wide), 0)
    km_aug = jnp.concatenate([km_ref[...], kbm_ref[...]], axis=1)
    kd_aug = keys(diag_start, wide)
    s_meta = [jnp.where(meta_mask, _nt_dot(q_aug[hh], km_aug), MASK_VALUE) for hh in two]
    s_diag = [jnp.where(causal, _nt_dot(q_aug[hh], kd_aug), MASK_VALUE) for hh in two]

    n_pairs = (n_full + 1) // 2

    def block_start(t):
        return pl.multiple_of(jnp.clip(t, 0, jnp.maximum(n_full - 1, 0)) * wide, wide)

    def value_start(t):
        return pl.multiple_of(jnp.where(t < 0, diag_start, block_start(t)), wide)

    def pipeline_step(u, carry):
        one_step(2 * u, 0, None)
        one_step(2 * u + 1, 1, 2 * u + 1 < n_full)
        return carry

    def one_step(t, cur, real_block):
        nxt = 1 - cur
        prev_start = value_start(t - 1)
        next_aug = keys(block_start(t + 1), wide)
        for hh in two:
            acc_scr[hh] = alpha_scr[hh] * acc_scr[hh] + _dot(p_scr[nxt, hh], values(prev_start, wide, hh))
        for hh in two:
            m_old = m_scr[hh]
            m_new = jnp.maximum(m_old, jnp.max(s_scr[cur, hh], axis=-1, keepdims=True))
            if real_block is not None:
                m_new = jnp.where(real_block, m_new, m_old)
            alpha_scr[hh] = jnp.exp2(m_old - m_new)
            m_scr[hh] = m_new
            p_scr[cur, hh] = jnp.exp2(s_scr[cur, hh] - m_new).astype(BF16)
        for hh in two:
            s_scr[nxt, hh] = _nt_dot(q_aug[hh], next_aug)

    first_aug = keys(block_start(0), wide)
    for hh in two:
        s_scr[0, hh] = _nt_dot(q_aug[hh], first_aug)
    for hh in two:
        m_first = jnp.maximum(jnp.max(s_meta[hh], axis=-1, keepdims=True),
                              jnp.max(s_diag[hh], axis=-1, keepdims=True))
        m_scr[hh] = m_first
        alpha_scr[hh] = jnp.ones((tq, 1), F32)
        p_scr[1, hh] = jnp.exp2(s_diag[hh] - m_first).astype(BF16)
        acc_scr[hh] = _dot(jnp.exp2(s_meta[hh] - m_first).astype(BF16), jnp.where(own[hh], vm_ref[...], one))
    lax.fori_loop(0, n_pairs, pipeline_step, 0)

    t_last = 2 * n_pairs - 1
    last_start = value_start(t_last)
    keep_last = jnp.where(t_last < n_full, 1.0, 0.0)
    outs = []
    for hh in two:
        acc = alpha_scr[hh] * acc_scr[hh] + keep_last * _dot(p_scr[1, hh], values(last_start, wide, hh))
        outs.append(acc / pltpu.roll(acc, FOX_DH, 1))
    o_ref[...] = jnp.where(own[0], outs[0], outs[1]).astype(o_ref.dtype)


def _fox(fq, qb, fk, kb, fv, km, kbm, vm, *, batch, seq, tq):
    nq = seq // tq
    npair = FOX_HEADS // 2
    body = functools.partial(_fox_body, tq=tq)
    qspec = pl.BlockSpec((tq, LANES), lambda b, p, i: (b * nq + i, p))
    kspec = pl.BlockSpec((seq, LANES), lambda b, p, i: (b, p))
    mspec = pl.BlockSpec((LANES, LANES), lambda b, p, i: (0, p))
    return pl.pallas_call(
        body,
        grid=(batch, npair, nq),
        in_specs=[qspec, qspec, kspec, kspec, kspec, mspec, mspec, mspec],
        out_specs=qspec,
        out_shape=jax.ShapeDtypeStruct((batch * seq, FOX_HD), BF16),
        scratch_shapes=[pltpu.VMEM((2, 2, tq, tq), F32), pltpu.VMEM((2, 2, tq, tq), BF16),
                        pltpu.VMEM((2, tq, 1), F32), pltpu.VMEM((2, tq, 1), F32), pltpu.VMEM((2, tq, LANES), F32)],
        compiler_params=pltpu.CompilerParams(dimension_semantics=("arbitrary", "arbitrary", "arbitrary"),
                                             vmem_limit_bytes=VMEM_LIMIT),
        name="fox",
    )(fq, qb, fk, kb, fv, km, kbm, vm)


def _ffn_body(x_ref, og_ref, of_ref, wo_ref, nw_ref, wg_ref, wu_ref, wd_ref, fw_ref, out_ref, act_ref, *, fchunk):
    h1 = x_ref[...] + _dot(og_ref[...], wo_ref[0:GDN_HD, :]) + _dot(of_ref[...], wo_ref[GDN_HD:GDN_HD + FOX_HD, :])
    n = (h1 * lax.rsqrt(jnp.mean(h1 * h1, axis=-1, keepdims=True) + EPS) * nw_ref[...]).astype(BF16)
    d_ff = wg_ref.shape[1]
    for c in range(d_ff // fchunk):
        sl = slice(c * fchunk, (c + 1) * fchunk)
        g = _dot(n, wg_ref[:, sl])
        up = _dot(n, wu_ref[:, sl])
        act_ref[:, sl] = (_silu(g) * up).astype(BF16)
    acc = h1 + _dot(act_ref[...], wd_ref[...])
    out_ref[...] = acc * lax.rsqrt(jnp.mean(acc * acc, axis=-1, keepdims=True) + EPS) * fw_ref[...]


def _ffn(x2d, og, of, wo, nw, wg, wu, wd, fw, tm):
    rows = x2d.shape[0]
    row_spec = lambda n: pl.BlockSpec((tm, n), lambda i: (i, 0))
    const_spec = lambda a: pl.BlockSpec(a.shape, lambda i: (0, 0), pipeline_mode=pl.Buffered(1))
    assert wg.shape[1] % FFN_COLS == 0
    body = functools.partial(_ffn_body, fchunk=FFN_COLS)
    return pl.pallas_call(
        body,
        grid=(rows // tm,),
        in_specs=[row_spec(D_MODEL), row_spec(GDN_HD), row_spec(FOX_HD), const_spec(wo), const_spec(nw),
                  const_spec(wg), const_spec(wu), const_spec(wd), const_spec(fw)],
        out_specs=row_spec(D_MODEL),
        out_shape=jax.ShapeDtypeStruct((rows, D_MODEL), F32),
        scratch_shapes=[pltpu.VMEM((tm, wg.shape[1]), BF16)],
        compiler_params=pltpu.CompilerParams(dimension_semantics=("arbitrary",), vmem_limit_bytes=VMEM_LIMIT),
        name="ffn",
    )(x2d, og, of, wo, nw, wg, wu, wd, fw)


def _lane_vec(vals, offset):
    return jnp.zeros((1, LANES), F32).at[0, offset:offset + vals.shape[0]].set(vals.astype(F32))


def kernel(x, meta_tokens, attn_norm_w, w_in, conv_w, a_log, dt_bias, gdn_norm_w, fgate_b, w_out, ffn_norm_w,
           w_gate, w_up, w_down, final_norm_w):
    batch, seq, _ = x.shape
    assert w_in.shape[0] == 1, "one layer"
    for tile_rows in (INPROJ_ROWS, GDN_ROWS, CUMGATE_ROWS, FOX_ROWS, FFN_ROWS):
        assert seq % tile_rows == 0
    w = w_in[0]
    gate_start = GDN_QKV + GDN_HD
    fox_start = gate_start + 2 * GDN_HEADS
    forget_start = fox_start + 3 * FOX_HD
    assert w.shape == (D_MODEL, forget_start + FOX_HEADS)
    wbig = jnp.concatenate([w[:, :gate_start], w[:, fox_start:forget_start]], axis=1).astype(BF16)
    wsm = jnp.concatenate([w[:, gate_start:fox_start], w[:, forget_start:],
                           jnp.zeros((D_MODEL, LANES - 2 * GDN_HEADS - FOX_HEADS), F32)], axis=1).astype(BF16)
    anw = attn_norm_w[0][None, :]
    avec = _lane_vec(-jnp.exp(a_log[0]), LANE_DECAY)
    dtvec = _lane_vec(dt_bias[0], LANE_DECAY)
    fbvec = _lane_vec(fgate_b[0], LANE_FORGET)
    gnw = gdn_norm_w[0][None, :]
    convw = conv_w[0]
    bias_consts = _bias_constants()

    x2d = x.reshape(batch * seq, D_MODEL)

    zeros_hist = jnp.zeros((SUBLANES, GDN_QKV), F32)
    act_m, z_m, _, fk_m, fv_m, sm_m, tail_m = _inproj(meta_tokens, anw, wbig, wsm, convw, zeros_hist,
                                                      tm=N_META, tiles_per_seq=1)
    zeros_state = jnp.zeros((GDN_HEADS, GDN_D, GDN_D), F32)
    _, s_meta = _gdn(act_m, z_m, sm_m, zeros_state, avec, dtvec, gnw,
                     batch=1, seq=N_META, rows=N_META, chunk=N_META)
    ccol_m, _, kb_m = _cumgate(sm_m, fbvec, jnp.zeros((1, LANES), F32), bias_consts, batch=1, seq=N_META)
    pad_rows = ((0, LANES - N_META), (0, 0))
    km = jnp.pad(fk_m, pad_rows)
    kbm = jnp.pad(kb_m, pad_rows)
    vm = jnp.pad(fv_m, pad_rows)

    act, z, fq, fk, fv, sm, _ = _inproj(x2d, anw, wbig, wsm, convw, tail_m,
                                        tm=INPROJ_ROWS, tiles_per_seq=seq // INPROJ_ROWS)
    o_gdn, _ = _gdn(act, z, sm, s_meta[0], avec, dtvec, gnw, batch=batch, seq=seq, rows=GDN_ROWS, chunk=GDN_CHUNK)
    _, qb, kb = _cumgate(sm, fbvec, ccol_m[N_META - 1:], bias_consts, batch=batch, seq=seq)
    o_fox = _fox(fq, qb, fk, kb, fv, km, kbm, vm, batch=batch, seq=seq, tq=FOX_ROWS)

    out = _ffn(x2d, o_gdn, o_fox, w_out[0].astype(BF16), ffn_norm_w[0][None, :], w_gate[0].astype(BF16),
               w_up[0].astype(BF16), w_down[0].astype(BF16), final_norm_w[None, :], tm=FFN_ROWS)
    return out.reshape(batch, seq, D_MODEL)
```

```python
import functools
import math

import numpy as np
import jax
import jax.numpy as jnp
from jax import lax
from jax.experimental import pallas as pl
from jax.experimental.pallas import tpu as pltpu

F32 = jnp.float32
BF16 = jnp.bfloat16

D_MODEL = 1024
N_META = 16
GDN_HEADS = 4
GDN_D = 128
FOX_HEADS = 8
FOX_DH = 64
CONV_K = 4
EPS = 1e-6
MASK_VALUE = -1e30
LOG2E = 1.4426950408889634
GDN_QKV = 3 * GDN_HEADS * GDN_D
GDN_HD = GDN_HEADS * GDN_D
FOX_HD = FOX_HEADS * FOX_DH
LANES = 128
SUBLANES = 8
LANE_BETA = 0
LANE_DECAY = GDN_HEADS
LANE_FORGET = 2 * GDN_HEADS
BIAS_STRIDE = 8
INV_BASE = 16
CONV_SLICE = 512
VMEM_LIMIT = 56 * 1024 * 1024
INPROJ_ROWS = 1024
GDN_ROWS = 512
GDN_CHUNK = 128
CUMGATE_ROWS = 512
FOX_ROWS = 512
FFN_ROWS = 1024
FFN_COLS = 256


def _nt_dot(a, b):
    return lax.dot_general(a, b, (((1,), (1,)), ((), ())), preferred_element_type=F32)


def _tn_dot(a, b):
    return lax.dot_general(a, b, (((0,), (0,)), ((), ())), preferred_element_type=F32)


def _dot(a, b):
    return jnp.dot(a, b, preferred_element_type=F32)


def _split3(x):
    x1 = x.astype(BF16)
    r1 = x - x1.astype(F32)
    x2 = r1.astype(BF16)
    x3 = (r1 - x2.astype(F32)).astype(BF16)
    return x1, x2, x3


def _softplus(x):
    return jnp.maximum(x, 0.0) + jnp.log1p(jnp.exp(-jnp.abs(x)))


def _sigmoid(x):
    return 1.0 / (1.0 + jnp.exp(-x))


def _silu(x):
    return x * _sigmoid(x)


def _inproj_body(x_ref, nw_ref, wbig_ref, wsm_ref, convw_ref, hist_ref,
                 act_ref, z_ref, fq_ref, fk_ref, fv_ref, sm_ref, tail_ref, xpad_ref, *, tm, tiles_per_seq):
    @pl.when(pl.program_id(0) % tiles_per_seq == 0)
    def _():
        xpad_ref[0:SUBLANES, :] = hist_ref[...]

    x = x_ref[...]
    u = (x * lax.rsqrt(jnp.mean(x * x, axis=-1, keepdims=True) + EPS) * nw_ref[...]).astype(BF16)
    o0 = GDN_QKV
    o1 = o0 + GDN_HD
    o2 = o1 + FOX_HD
    o3 = o2 + FOX_HD
    o4 = o3 + FOX_HD

    def proj_z():
        z_ref[...] = _dot(u, wbig_ref[:, o0:o1])

    def proj_fq():
        fq_ref[...] = (_dot(u, wbig_ref[:, o1:o2]) * (LOG2E * FOX_DH ** -0.5)).astype(BF16)

    def proj_fk():
        fk_ref[...] = _dot(u, wbig_ref[:, o2:o3]).astype(BF16)

    def proj_fv():
        fv_ref[...] = _dot(u, wbig_ref[:, o3:o4]).astype(BF16)

    def proj_sm():
        sm_ref[...] = _dot(u, wsm_ref[...])

    others = iter([proj_z, proj_fq, proj_fk, proj_fv, proj_sm])
    first = SUBLANES - (CONV_K - 1)
    for lo in range(0, GDN_QKV, CONV_SLICE):
        hi = lo + CONV_SLICE
        xpad_ref[SUBLANES:SUBLANES + tm, lo:hi] = _dot(u, wbig_ref[:, lo:hi])
        nxt = next(others, None)
        if nxt is not None:
            nxt()
        conv = convw_ref[0:1, lo:hi] * xpad_ref[first:first + tm, lo:hi]
        for kk in range(1, CONV_K):
            conv = conv + convw_ref[kk:kk + 1, lo:hi] * xpad_ref[first + kk:first + kk + tm, lo:hi]
        act_ref[:, lo:hi] = _silu(conv)
    for nxt in others:
        nxt()
    tail_ref[...] = xpad_ref[tm:tm + SUBLANES, :]
    xpad_ref[0:SUBLANES, :] = xpad_ref[tm:tm + SUBLANES, :]


def _inproj(x2d, nw, wbig, wsm, convw, hist, *, tm, tiles_per_seq):
    rows = x2d.shape[0]
    ntiles = rows // tm
    row_spec = lambda n: pl.BlockSpec((tm, n), lambda i: (i, 0))
    const_spec = lambda a: pl.BlockSpec(a.shape, lambda i: (0, 0))
    body = functools.partial(_inproj_body, tm=tm, tiles_per_seq=tiles_per_seq)
    return pl.pallas_call(
        body,
        grid=(ntiles,),
        in_specs=[row_spec(D_MODEL), const_spec(nw), const_spec(wbig), const_spec(wsm), const_spec(convw),
                  const_spec(hist)],
        out_specs=[row_spec(GDN_QKV), row_spec(GDN_HD), row_spec(FOX_HD), row_spec(FOX_HD), row_spec(FOX_HD),
                   row_spec(LANES), pl.BlockSpec((SUBLANES, GDN_QKV), lambda i: (i, 0))],
        out_shape=[jax.ShapeDtypeStruct((rows, GDN_QKV), F32), jax.ShapeDtypeStruct((rows, GDN_HD), F32),
                   jax.ShapeDtypeStruct((rows, FOX_HD), BF16), jax.ShapeDtypeStruct((rows, FOX_HD), BF16),
                   jax.ShapeDtypeStruct((rows, FOX_HD), BF16), jax.ShapeDtypeStruct((rows, LANES), F32),
                   jax.ShapeDtypeStruct((ntiles * SUBLANES, GDN_QKV), F32)],
        scratch_shapes=[pltpu.VMEM((tm + SUBLANES, GDN_QKV), F32)],
        compiler_params=pltpu.CompilerParams(dimension_semantics=("arbitrary",), vmem_limit_bytes=VMEM_LIMIT),
        name="inproj",
    )(x2d, nw, wbig, wsm, convw, hist)


def _unit_lower_inverses(a_mats, r_i, c_i, size):
    n = range(len(a_mats))
    base = min(INV_BASE, size)
    eye = jnp.where(r_i == c_i, 1.0, 0.0).astype(F32)
    rc_xor = r_i ^ c_i
    p_pow = [jnp.where(rc_xor < base, -a_mats[i], 0.0) for i in n]
    t_inv = [eye + p_pow[i] for i in n]
    for _ in range(int(math.log2(base)) - 1):
        p16 = [p_pow[i].astype(BF16) for i in n]
        p_pow = [_dot(p16[i], p16[i]) for i in n]
        t_inv = [t_inv[i] + _dot(t_inv[i].astype(BF16), p_pow[i].astype(BF16)) for i in n]
    half = base
    while half < size:
        lower_left = (rc_xor >= half) & (rc_xor < 2 * half)
        t16 = [t_inv[i].astype(BF16) for i in n]
        y16 = [_dot(jnp.where(lower_left, a_mats[i], 0.0).astype(BF16), t16[i]).astype(BF16) for i in n]
        t_inv = [t_inv[i] - _dot(t16[i], y16[i]) for i in n]
        half *= 2
    return t_inv


def _gdn_body(act_ref, z_ref, sm_ref, s0_ref, avec_ref, dtvec_ref, normw_ref, o_ref, sout_ref, state_ref,
              *, rows, chunk):
    j = pl.program_id(1)

    @pl.when(j == 0)
    def _():
        state_ref[...] = s0_ref[...]

    sm = sm_ref[...]
    beta_all = _sigmoid(sm)
    g_all = avec_ref[...] * _softplus(sm + dtvec_ref[...])

    rr = lax.broadcasted_iota(jnp.int32, (rows, rows), 0)
    cc = lax.broadcasted_iota(jnp.int32, (rows, rows), 1)
    tri = jnp.where((rr >= cc) & ((rr ^ cc) < chunk), 1.0, 0.0).astype(BF16)
    g1, g2, g3 = _split3(g_all)
    gc_all = _dot(tri, g1) + _dot(tri, g2) + _dot(tri, g3)
    sel = jnp.where(lax.broadcasted_iota(jnp.int32, (SUBLANES, LANES), 1)
                    == lax.broadcasted_iota(jnp.int32, (SUBLANES, LANES), 0) + LANE_DECAY, 1.0, 0.0).astype(BF16)
    c1, c2, c3 = _split3(gc_all)
    gc_rows = _nt_dot(sel, c1) + _nt_dot(sel, c2) + _nt_dot(sel, c3)
    egc_all = jnp.exp(gc_all)
    normw = normw_ref[...]
    r_i = lax.broadcasted_iota(jnp.int32, (chunk, chunk), 0)
    c_i = lax.broadcasted_iota(jnp.int32, (chunk, chunk), 1)
    incl = r_i >= c_i
    strict = r_i > c_i

    heads = range(GDN_HEADS)
    pairs = [(c, h) for c in range(rows // chunk) for h in heads]
    gl_rows = [gc_all[(c + 1) * chunk - 1:(c + 1) * chunk, :] for c in range(rows // chunk)]
    ekg_all = [jnp.exp(gl_rows[c] - gc_all[c * chunk:(c + 1) * chunk, :]) for c in range(rows // chunk)]
    egl_all = [jnp.exp(gl_rows[c]) for c in range(rows // chunk)]
    qgs, kgs, rhss, a_mats, qks, egls = [], [], [], [], [], []
    for c, h in pairs:
        lo = c * chunk
        hi = lo + chunk
        q = act_ref[lo:hi, h * GDN_D:(h + 1) * GDN_D]
        k = act_ref[lo:hi, GDN_HD + h * GDN_D:GDN_HD + (h + 1) * GDN_D]
        v = act_ref[lo:hi, 2 * GDN_HD + h * GDN_D:2 * GDN_HD + (h + 1) * GDN_D]
        q = q * lax.rsqrt(jnp.sum(q * q, axis=-1, keepdims=True) + EPS) * (GDN_D ** -0.5)
        k = k * lax.rsqrt(jnp.sum(k * k, axis=-1, keepdims=True) + EPS)
        beta = beta_all[lo:hi, LANE_BETA + h:LANE_BETA + h + 1]
        gcol = gc_all[lo:hi, LANE_DECAY + h:LANE_DECAY + h + 1]
        egc = egc_all[lo:hi, LANE_DECAY + h:LANE_DECAY + h + 1]
        grow = gc_rows[h:h + 1, lo:hi]
        kb = k * beta
        k16 = k.astype(BF16)
        decay = jnp.where(incl, jnp.exp(jnp.where(incl, gcol - grow, 0.0)), 0.0)
        qgs.append(q * egc)
        kgs.append((k * ekg_all[c][:, LANE_DECAY + h:LANE_DECAY + h + 1]).astype(BF16))
        egls.append(egl_all[c][:, LANE_DECAY + h:LANE_DECAY + h + 1])
        rhss.append(jnp.concatenate([v * beta, kb * egc], axis=1).astype(BF16))
        a_mats.append(jnp.where(strict, _nt_dot(kb.astype(BF16), k16) * decay, 0.0))
        qks.append(jnp.where(incl, _nt_dot(q.astype(BF16), k16) * decay, 0.0).astype(BF16))
    t_invs = _unit_lower_inverses(a_mats, r_i, c_i, chunk)
    uws = [_dot(t_invs[i].astype(BF16), rhss[i]) for i in range(len(pairs))]

    state = [state_ref[h] for h in heads]
    for c in range(rows // chunk):
        idx = [c * GDN_HEADS + h for h in heads]
        wss = [_dot(jnp.concatenate([uws[i][:, GDN_D:], qgs[i]], axis=0).astype(BF16), state[h].astype(BF16))
               for h, i in zip(heads, idx)]
        v_news = [(uws[i][:, :GDN_D] - ws[:chunk]).astype(BF16) for i, ws in zip(idx, wss)]
        outs = [ws[chunk:] + _dot(qks[i], v_new) for i, ws, v_new in zip(idx, wss, v_news)]
        state = [state[h] * egls[i] + _tn_dot(kgs[i], v_new) for h, i, v_new in zip(heads, idx, v_news)]
        for h in heads:
            zg = _silu(z_ref[c * chunk:(c + 1) * chunk, h * GDN_D:(h + 1) * GDN_D])
            o = outs[h]
            o = o * lax.rsqrt(jnp.mean(o * o, axis=-1, keepdims=True) + EPS) * normw * zg
            o_ref[c * chunk:(c + 1) * chunk, h * GDN_D:(h + 1) * GDN_D] = o.astype(o_ref.dtype)
    for h in heads:
        state_ref[h] = state[h]

    @pl.when(j == pl.num_programs(1) - 1)
    def _():
        sout_ref[0] = state_ref[...]


def _gdn(act, z, sm, s0, avec, dtvec, normw, *, batch, seq, rows, chunk):
    nblk = seq // rows
    row_spec = lambda n: pl.BlockSpec((rows, n), lambda b, j: (b * nblk + j, 0))
    const2 = lambda a: pl.BlockSpec(a.shape, lambda b, j: (0, 0))
    body = functools.partial(_gdn_body, rows=rows, chunk=chunk)
    return pl.pallas_call(
        body,
        grid=(batch, nblk),
        in_specs=[row_spec(GDN_QKV), row_spec(GDN_HD), row_spec(LANES),
                  pl.BlockSpec(s0.shape, lambda b, j: (0, 0, 0)), const2(avec), const2(dtvec), const2(normw)],
        out_specs=[row_spec(GDN_HD), pl.BlockSpec((1, GDN_HEADS, GDN_D, GDN_D), lambda b, j: (b, 0, 0, 0))],
        out_shape=[jax.ShapeDtypeStruct((batch * seq, GDN_HD), BF16),
                   jax.ShapeDtypeStruct((batch, GDN_HEADS, GDN_D, GDN_D), F32)],
        scratch_shapes=[pltpu.VMEM((GDN_HEADS, GDN_D, GDN_D), F32)],
        compiler_params=pltpu.CompilerParams(dimension_semantics=("arbitrary", "arbitrary"),
                                             vmem_limit_bytes=VMEM_LIMIT),
        name="gdn",
    )(act, z, sm, s0, avec, dtvec, normw)


def _bias_constants():
    sel = np.zeros((LANES, 2 * FOX_HD), np.float32)
    ones = np.zeros((1, 2 * FOX_HD), np.float32)
    for head in range(FOX_HEADS):
        base = (head // 2) * LANES + BIAS_STRIDE * (head % 2)
        for t in range(3):
            src = LANE_FORGET + t * FOX_HEADS + head
            sel[src, base + t] = 1.0
            sel[src, FOX_HD + base + 3 + t] = -1.0
            ones[0, base + 3 + t] = 1.0
            ones[0, FOX_HD + base + t] = 1.0
    return jnp.asarray(sel, BF16), jnp.asarray(ones)


def _pack3(x):
    hi = x.astype(BF16).astype(F32)
    rest = x - hi
    mid = rest.astype(BF16).astype(F32)
    lo = (rest - mid).astype(BF16).astype(F32)
    return (hi + pltpu.roll(mid, FOX_HEADS, 1) + pltpu.roll(lo, 2 * FOX_HEADS, 1)).astype(BF16)


def _cumgate_body(sm_ref, fb_ref, c0_ref, sel_ref, ones_ref, ccol_ref, qb_ref, kb_ref, *, seq, blk):
    lane = lax.broadcasted_iota(jnp.int32, (1, LANES), 1)
    valid = (lane >= LANE_FORGET) & (lane < LANE_FORGET + FOX_HEADS)
    r_i = lax.broadcasted_iota(jnp.int32, (blk, blk), 0)
    c_i = lax.broadcasted_iota(jnp.int32, (blk, blk), 1)
    tri = jnp.where(r_i >= c_i, 1.0, 0.0).astype(BF16)
    carry = c0_ref[...]
    for i in range(seq // blk):
        xg = sm_ref[i * blk:(i + 1) * blk, :] + fb_ref[...]
        logf = jnp.where(valid, jnp.minimum(xg, 0.0) - jnp.log1p(jnp.exp(-jnp.abs(xg))), 0.0)
        cs3 = _dot(tri, _pack3(logf))
        cs = cs3 + pltpu.roll(cs3, LANES - FOX_HEADS, 1) + pltpu.roll(cs3, LANES - 2 * FOX_HEADS, 1)
        cs = jnp.where(valid, carry + cs, 0.0)
        ccol_ref[i * blk:(i + 1) * blk, :] = cs
        bias = _dot(_pack3(cs * LOG2E), sel_ref[...]) + ones_ref[...]
        qb_ref[i * blk:(i + 1) * blk, :] = bias[:, :FOX_HD].astype(BF16)
        kb_ref[i * blk:(i + 1) * blk, :] = bias[:, FOX_HD:].astype(BF16)
        carry = cs[blk - 1:blk, :]


def _cumgate(sm, fb, c0, consts, *, batch, seq):
    blk = min(CUMGATE_ROWS, seq)
    sel, ones = consts
    body = functools.partial(_cumgate_body, seq=seq, blk=blk)
    c2 = lambda a: pl.BlockSpec(a.shape, lambda b: (0, 0))
    return pl.pallas_call(
        body,
        grid=(batch,),
        in_specs=[pl.BlockSpec((seq, LANES), lambda b: (b, 0)), c2(fb), c2(c0), c2(sel), c2(ones)],
        out_specs=[pl.BlockSpec((seq, LANES), lambda b: (b, 0)), pl.BlockSpec((seq, FOX_HD), lambda b: (b, 0)),
                   pl.BlockSpec((seq, FOX_HD), lambda b: (b, 0))],
        out_shape=[jax.ShapeDtypeStruct((batch * seq, LANES), F32),
                   jax.ShapeDtypeStruct((batch * seq, FOX_HD), BF16),
                   jax.ShapeDtypeStruct((batch * seq, FOX_HD), BF16)],
        compiler_params=pltpu.CompilerParams(dimension_semantics=("arbitrary",), vmem_limit_bytes=VMEM_LIMIT),
        name="cumgate",
    )(sm, fb, c0, sel, ones)


def _fox_body(q_ref, qb_ref, k_ref, kb_ref, v_ref, km_ref, kbm_ref, vm_ref, o_ref,
              s_scr, p_scr, m_scr, alpha_scr, acc_scr, *, tq):
    i = pl.program_id(2)
    wide = tq
    lane = lax.broadcasted_iota(jnp.int32, (1, LANES), 1)
    q_all = q_ref[...]
    qb_all = qb_ref[...]
    zero = jnp.zeros_like(q_all)
    one = jnp.ones_like(q_all[0:1, :])
    own = [lane < FOX_DH, lane >= FOX_DH]
    q_aug = []
    for hh in range(2):
        bias_lanes = (lane >= BIAS_STRIDE * hh) & (lane < BIAS_STRIDE * (hh + 1))
        q_aug.append(jnp.concatenate([jnp.where(own[hh], q_all, zero), jnp.where(bias_lanes, qb_all, zero)], axis=1))

    def keys(start, width):
        return jnp.concatenate([k_ref[pl.ds(start, width), :], kb_ref[pl.ds(start, width), :]], axis=1)

    def values(start, width, hh):
        return jnp.where(own[hh], v_ref[pl.ds(start, width), :], one)

    n_full = i
    two = range(2)

    diag_start = pl.multiple_of(n_full * wide, wide)
    meta_mask = lax.broadcasted_iota(jnp.int32, (tq, LANES), 1) < N_META
    causal = lax.broadcasted_iota(jnp.int32, (tq, wide), 1) <= lax.broadcasted_iota(jnp.int32, (tq, wide), 0)
    km_aug = jnp.concatenate([km_ref[...], kbm_ref[...]], axis=1)
    kd_aug = keys(diag_start, wide)
    s_meta = [jnp.where(meta_mask, _nt_dot(q_aug[hh], km_aug), MASK_VALUE) for hh in two]
    s_diag = [jnp.where(causal, _nt_dot(q_aug[hh], kd_aug), MASK_VALUE) for hh in two]

    n_pairs = (n_full + 1) // 2

    def block_start(t):
        return pl.multiple_of(jnp.clip(t, 0, jnp.maximum(n_full - 1, 0)) * wide, wide)

    def value_start(t):
        return pl.multiple_of(jnp.where(t < 0, diag_start, block_start(t)), wide)

    def pipeline_step(u, carry):
        one_step(2 * u, 0, None)
        one_step(2 * u + 1, 1, 2 * u + 1 < n_full)
        return carry

    def one_step(t, cur, real_block):
        nxt = 1 - cur
        prev_start = value_start(t - 1)
        next_aug = keys(block_start(t + 1), wide)
        for hh in two:
            acc_scr[hh] = alpha_scr[hh] * acc_scr[hh] + _dot(p_scr[nxt, hh], values(prev_start, wide, hh))
        for hh in two:
            m_old = m_scr[hh]
            m_new = jnp.maximum(m_old, jnp.max(s_scr[cur, hh], axis=-1, keepdims=True))
            if real_block is not None:
                m_new = jnp.where(real_block, m_new, m_old)
            alpha_scr[hh] = jnp.exp2(m_old - m_new)
            m_scr[hh] = m_new
            p_scr[cur, hh] = jnp.exp2(s_scr[cur, hh] - m_new).astype(BF16)
        for hh in two:
            s_scr[nxt, hh] = _nt_dot(q_aug[hh], next_aug)

    first_aug = keys(block_start(0), wide)
    for hh in two:
        s_scr[0, hh] = _nt_dot(q_aug[hh], first_aug)
    for hh in two:
        m_first = jnp.maximum(jnp.max(s_meta[hh], axis=-1, keepdims=True),
                              jnp.max(s_diag[hh], axis=-1, keepdims=True))
        m_scr[hh] = m_first
        alpha_scr[hh] = jnp.ones((tq, 1), F32)
        p_scr[1, hh] = jnp.exp2(s_diag[hh] - m_first).astype(BF16)
        acc_scr[hh] = _dot(jnp.exp2(s_meta[hh] - m_first).astype(BF16), jnp.where(own[hh], vm_ref[...], one))
    lax.fori_loop(0, n_pairs, pipeline_step, 0)

    t_last = 2 * n_pairs - 1
    last_start = value_start(t_last)
    keep_last = jnp.where(t_last < n_full, 1.0, 0.0)
    outs = []
    for hh in two:
        acc = alpha_scr[hh] * acc_scr[hh] + keep_last * _dot(p_scr[1, hh], values(last_start, wide, hh))
        outs.append(acc / pltpu.roll(acc, FOX_DH, 1))
    o_ref[...] = jnp.where(own[0], outs[0], outs[1]).astype(o_ref.dtype)


def _fox(fq, qb, fk, kb, fv, km, kbm, vm, *, batch, seq, tq):
    nq = seq // tq
    npair = FOX_HEADS // 2
    body = functools.partial(_fox_body, tq=tq)
    qspec = pl.BlockSpec((tq, LANES), lambda b, p, i: (b * nq + i, p))
    kspec = pl.BlockSpec((seq, LANES), lambda b, p, i: (b, p))
    mspec = pl.BlockSpec((LANES, LANES), lambda b, p, i: (0, p))
    return pl.pallas_call(
        body,
        grid=(batch, npair, nq),
        in_specs=[qspec, qspec, kspec, kspec, kspec, mspec, mspec, mspec],
        out_specs=qspec,
        out_shape=jax.ShapeDtypeStruct((batch * seq, FOX_HD), BF16),
        scratch_shapes=[pltpu.VMEM((2, 2, tq, tq), F32), pltpu.VMEM((2, 2, tq, tq), BF16),
                        pltpu.VMEM((2, tq, 1), F32), pltpu.VMEM((2, tq, 1), F32), pltpu.VMEM((2, tq, LANES), F32)],
        compiler_params=pltpu.CompilerParams(dimension_semantics=("arbitrary", "arbitrary", "arbitrary"),
                                             vmem_limit_bytes=VMEM_LIMIT),
        name="fox",
    )(fq, qb, fk, kb, fv, km, kbm, vm)


def _ffn_body(x_ref, og_ref, of_ref, wo_ref, nw_ref, wg_ref, wu_ref, wd_ref, fw_ref, out_ref, act_ref, *, fchunk):
    h1 = x_ref[...] + _dot(og_ref[...], wo_ref[0:GDN_HD, :]) + _dot(of_ref[...], wo_ref[GDN_HD:GDN_HD + FOX_HD, :])
    n = (h1 * lax.rsqrt(jnp.mean(h1 * h1, axis=-1, keepdims=True) + EPS) * nw_ref[...]).astype(BF16)
    d_ff = wg_ref.shape[1]
    for c in range(d_ff // fchunk):
        sl = slice(c * fchunk, (c + 1) * fchunk)
        g = _dot(n, wg_ref[:, sl])
        up = _dot(n, wu_ref[:, sl])
        act_ref[:, sl] = (_silu(g) * up).astype(BF16)
    acc = h1 + _dot(act_ref[...], wd_ref[...])
    out_ref[...] = acc * lax.rsqrt(jnp.mean(acc * acc, axis=-1, keepdims=True) + EPS) * fw_ref[...]


def _ffn(x2d, og, of, wo, nw, wg, wu, wd, fw, tm):
    rows = x2d.shape[0]
    row_spec = lambda n: pl.BlockSpec((tm, n), lambda i: (i, 0))
    const_spec = lambda a: pl.BlockSpec(a.shape, lambda i: (0, 0), pipeline_mode=pl.Buffered(1))
    assert wg.shape[1] % FFN_COLS == 0
    body = functools.partial(_ffn_body, fchunk=FFN_COLS)
    return pl.pallas_call(
        body,
        grid=(rows // tm,),
        in_specs=[row_spec(D_MODEL), row_spec(GDN_HD), row_spec(FOX_HD), const_spec(wo), const_spec(nw),
                  const_spec(wg), const_spec(wu), const_spec(wd), const_spec(fw)],
        out_specs=row_spec(D_MODEL),
        out_shape=jax.ShapeDtypeStruct((rows, D_MODEL), F32),
        scratch_shapes=[pltpu.VMEM((tm, wg.shape[1]), BF16)],
        compiler_params=pltpu.CompilerParams(dimension_semantics=("arbitrary",), vmem_limit_bytes=VMEM_LIMIT),
        name="ffn",
    )(x2d, og, of, wo, nw, wg, wu, wd, fw)


def _lane_vec(vals, offset):
    return jnp.zeros((1, LANES), F32).at[0, offset:offset + vals.shape[0]].set(vals.astype(F32))


def kernel(x, meta_tokens, attn_norm_w, w_in, conv_w, a_log, dt_bias, gdn_norm_w, fgate_b, w_out, ffn_norm_w,
           w_gate, w_up, w_down, final_norm_w):
    batch, seq, _ = x.shape
    assert w_in.shape[0] == 1, "one layer"
    for tile_rows in (INPROJ_ROWS, GDN_ROWS, CUMGATE_ROWS, FOX_ROWS, FFN_ROWS):
        assert seq % tile_rows == 0
    w = w_in[0]
    gate_start = GDN_QKV + GDN_HD
    fox_start = gate_start + 2 * GDN_HEADS
    forget_start = fox_start + 3 * FOX_HD
    assert w.shape == (D_MODEL, forget_start + FOX_HEADS)
    wbig = jnp.concatenate([w[:, :gate_start], w[:, fox_start:forget_start]], axis=1).astype(BF16)
    wsm = jnp.concatenate([w[:, gate_start:fox_start], w[:, forget_start:],
                           jnp.zeros((D_MODEL, LANES - 2 * GDN_HEADS - FOX_HEADS), F32)], axis=1).astype(BF16)
    anw = attn_norm_w[0][None, :]
    avec = _lane_vec(-jnp.exp(a_log[0]), LANE_DECAY)
    dtvec = _lane_vec(dt_bias[0], LANE_DECAY)
    fbvec = _lane_vec(fgate_b[0], LANE_FORGET)
    gnw = gdn_norm_w[0][None, :]
    convw = conv_w[0]
    bias_consts = _bias_constants()

    x2d = x.reshape(batch * seq, D_MODEL)

    zeros_hist = jnp.zeros((SUBLANES, GDN_QKV), F32)
    act_m, z_m, _, fk_m, fv_m, sm_m, tail_m = _inproj(meta_tokens, anw, wbig, wsm, convw, zeros_hist,
                                                      tm=N_META, tiles_per_seq=1)
    zeros_state = jnp.zeros((GDN_HEADS, GDN_D, GDN_D), F32)
    _, s_meta = _gdn(act_m, z_m, sm_m, zeros_state, avec, dtvec, gnw,
                     batch=1, seq=N_META, rows=N_META, chunk=N_META)
    ccol_m, _, kb_m = _cumgate(sm_m, fbvec, jnp.zeros((1, LANES), F32), bias_consts, batch=1, seq=N_META)
    pad_rows = ((0, LANES - N_META), (0, 0))
    km = jnp.pad(fk_m, pad_rows)
    kbm = jnp.pad(kb_m, pad_rows)
    vm = jnp.pad(fv_m, pad_rows)

    act, z, fq, fk, fv, sm, _ = _inproj(x2d, anw, wbig, wsm, convw, tail_m,
                                        tm=INPROJ_ROWS, tiles_per_seq=seq // INPROJ_ROWS)
    o_gdn, _ = _gdn(act, z, sm, s_meta[0], avec, dtvec, gnw, batch=batch, seq=seq, rows=GDN_ROWS, chunk=GDN_CHUNK)
    _, qb, kb = _cumgate(sm, fbvec, ccol_m[N_META - 1:], bias_consts, batch=batch, seq=seq)
    o_fox = _fox(fq, qb, fk, kb, fv, km, kbm, vm, batch=batch, seq=seq, tq=FOX_ROWS)

    out = _ffn(x2d, o_gdn, o_fox, w_out[0].astype(BF16), ffn_norm_w[0][None, :], w_gate[0].astype(BF16),
               w_up[0].astype(BF16), w_down[0].astype(BF16), final_norm_w[None, :], tm=FFN_ROWS)
    return out.reshape(batch, seq, D_MODEL)
```

```python
import functools
import math

import numpy as np
import jax
import jax.numpy as jnp
from jax import lax
from jax.experimental import pallas as pl
from jax.experimental.pallas import tpu as pltpu

F32 = jnp.float32
BF16 = jnp.bfloat16

D_MODEL = 1024
N_META = 16
GDN_HEADS = 4
GDN_D = 128
FOX_HEADS = 8
FOX_DH = 64
CONV_K = 4
EPS = 1e-6
MASK_VALUE = -1e30
LOG2E = 1.4426950408889634
GDN_QKV = 3 * GDN_HEADS * GDN_D
GDN_HD = GDN_HEADS * GDN_D
FOX_HD = FOX_HEADS * FOX_DH
LANES = 128
SUBLANES = 8
LANE_BETA = 0
LANE_DECAY = GDN_HEADS
LANE_FORGET = 2 * GDN_HEADS
BIAS_STRIDE = 8
INV_BASE = 16
CONV_SLICE = 512
VMEM_LIMIT = 56 * 1024 * 1024
INPROJ_ROWS = 1024
GDN_ROWS = 512
GDN_CHUNK = 128
CUMGATE_ROWS = 512
FOX_ROWS = 512
SOFTMAX_ROWS = 64
FFN_ROWS = 1024
FFN_COLS = 256


def _nt_dot(a, b):
    return lax.dot_general(a, b, (((1,), (1,)), ((), ())), preferred_element_type=F32)


def _tn_dot(a, b):
    return lax.dot_general(a, b, (((0,), (0,)), ((), ())), preferred_element_type=F32)


def _dot(a, b):
    return jnp.dot(a, b, preferred_element_type=F32)


def _split3(x):
    x1 = x.astype(BF16)
    r1 = x - x1.astype(F32)
    x2 = r1.astype(BF16)
    x3 = (r1 - x2.astype(F32)).astype(BF16)
    return x1, x2, x3


def _softplus(x):
    return jnp.maximum(x, 0.0) + jnp.log1p(jnp.exp(-jnp.abs(x)))


def _sigmoid(x):
    return 1.0 / (1.0 + jnp.exp(-x))


def _silu(x):
    return x * _sigmoid(x)


def _inproj_body(x_ref, nw_ref, wbig_ref, wsm_ref, convw_ref, hist_ref,
                 act_ref, z_ref, fq_ref, fk_ref, fv_ref, sm_ref, tail_ref, xpad_ref, *, tm, tiles_per_seq):
    @pl.when(pl.program_id(0) % tiles_per_seq == 0)
    def _():
        xpad_ref[0:SUBLANES, :] = hist_ref[...]

    x = x_ref[...]
    u = (x * lax.rsqrt(jnp.mean(x * x, axis=-1, keepdims=True) + EPS) * nw_ref[...]).astype(BF16)
    o0 = GDN_QKV
    o1 = o0 + GDN_HD
    o2 = o1 + FOX_HD
    o3 = o2 + FOX_HD
    o4 = o3 + FOX_HD

    def proj_z():
        z_ref[...] = _dot(u, wbig_ref[:, o0:o1])

    def proj_fq():
        fq_ref[...] = (_dot(u, wbig_ref[:, o1:o2]) * (LOG2E * FOX_DH ** -0.5)).astype(BF16)

    def proj_fk():
        fk_ref[...] = _dot(u, wbig_ref[:, o2:o3]).astype(BF16)

    def proj_fv():
        fv_ref[...] = _dot(u, wbig_ref[:, o3:o4]).astype(BF16)

    def proj_sm():
        sm_ref[...] = _dot(u, wsm_ref[...])

    others = iter([proj_z, proj_fq, proj_fk, proj_fv, proj_sm])
    first = SUBLANES - (CONV_K - 1)
    for lo in range(0, GDN_QKV, CONV_SLICE):
        hi = lo + CONV_SLICE
        xpad_ref[SUBLANES:SUBLANES + tm, lo:hi] = _dot(u, wbig_ref[:, lo:hi])
        nxt = next(others, None)
        if nxt is not None:
            nxt()
        conv = convw_ref[0:1, lo:hi] * xpad_ref[first:first + tm, lo:hi]
        for kk in range(1, CONV_K):
            conv = conv + convw_ref[kk:kk + 1, lo:hi] * xpad_ref[first + kk:first + kk + tm, lo:hi]
        act_ref[:, lo:hi] = _silu(conv)
    for nxt in others:
        nxt()
    tail_ref[...] = xpad_ref[tm:tm + SUBLANES, :]
    xpad_ref[0:SUBLANES, :] = xpad_ref[tm:tm + SUBLANES, :]


def _inproj(x2d, nw, wbig, wsm, convw, hist, *, tm, tiles_per_seq):
    rows = x2d.shape[0]
    ntiles = rows // tm
    row_spec = lambda n: pl.BlockSpec((tm, n), lambda i: (i, 0))
    const_spec = lambda a: pl.BlockSpec(a.shape, lambda i: (0, 0))
    body = functools.partial(_inproj_body, tm=tm, tiles_per_seq=tiles_per_seq)
    return pl.pallas_call(
        body,
        grid=(ntiles,),
        in_specs=[row_spec(D_MODEL), const_spec(nw), const_spec(wbig), const_spec(wsm), const_spec(convw),
                  const_spec(hist)],
        out_specs=[row_spec(GDN_QKV), row_spec(GDN_HD), row_spec(FOX_HD), row_spec(FOX_HD), row_spec(FOX_HD),
                   row_spec(LANES), pl.BlockSpec((SUBLANES, GDN_QKV), lambda i: (i, 0))],
        out_shape=[jax.ShapeDtypeStruct((rows, GDN_QKV), F32), jax.ShapeDtypeStruct((rows, GDN_HD), F32),
                   jax.ShapeDtypeStruct((rows, FOX_HD), BF16), jax.ShapeDtypeStruct((rows, FOX_HD), BF16),
                   jax.ShapeDtypeStruct((rows, FOX_HD), BF16), jax.ShapeDtypeStruct((rows, LANES), F32),
                   jax.ShapeDtypeStruct((ntiles * SUBLANES, GDN_QKV), F32)],
        scratch_shapes=[pltpu.VMEM((tm + SUBLANES, GDN_QKV), F32)],
        compiler_params=pltpu.CompilerParams(dimension_semantics=("arbitrary",), vmem_limit_bytes=VMEM_LIMIT),
        name="inproj",
    )(x2d, nw, wbig, wsm, convw, hist)


def _unit_lower_inverses(a_mats, r_i, c_i, size):
    n = range(len(a_mats))
    base = min(INV_BASE, size)
    eye = jnp.where(r_i == c_i, 1.0, 0.0).astype(F32)
    rc_xor = r_i ^ c_i
    p_pow = [jnp.where(rc_xor < base, -a_mats[i], 0.0) for i in n]
    t_inv = [eye + p_pow[i] for i in n]
    for _ in range(int(math.log2(base)) - 1):
        p16 = [p_pow[i].astype(BF16) for i in n]
        p_pow = [_dot(p16[i], p16[i]) for i in n]
        t_inv = [t_inv[i] + _dot(t_inv[i].astype(BF16), p_pow[i].astype(BF16)) for i in n]
    half = base
    while half < size:
        lower_left = (rc_xor >= half) & (rc_xor < 2 * half)
        t16 = [t_inv[i].astype(BF16) for i in n]
        y16 = [_dot(jnp.where(lower_left, a_mats[i], 0.0).astype(BF16), t16[i]).astype(BF16) for i in n]
        t_inv = [t_inv[i] - _dot(t16[i], y16[i]) for i in n]
        half *= 2
    return t_inv


def _gdn_body(act_ref, z_ref, sm_ref, s0_ref, avec_ref, dtvec_ref, normw_ref, o_ref, sout_ref, state_ref,
              *, rows, chunk):
    j = pl.program_id(1)

    @pl.when(j == 0)
    def _():
        state_ref[...] = s0_ref[...]

    sm = sm_ref[...]
    beta_all = _sigmoid(sm)
    g_all = avec_ref[...] * _softplus(sm + dtvec_ref[...])

    rr = lax.broadcasted_iota(jnp.int32, (rows, rows), 0)
    cc = lax.broadcasted_iota(jnp.int32, (rows, rows), 1)
    tri = jnp.where((rr >= cc) & ((rr ^ cc) < chunk), 1.0, 0.0).astype(BF16)
    g1, g2, g3 = _split3(g_all)
    gc_all = _dot(tri, g1) + _dot(tri, g2) + _dot(tri, g3)
    sel = jnp.where(lax.broadcasted_iota(jnp.int32, (SUBLANES, LANES), 1)
                    == lax.broadcasted_iota(jnp.int32, (SUBLANES, LANES), 0) + LANE_DECAY, 1.0, 0.0).astype(BF16)
    c1, c2, c3 = _split3(gc_all)
    gc_rows = _nt_dot(sel, c1) + _nt_dot(sel, c2) + _nt_dot(sel, c3)
    egc_all = jnp.exp(gc_all)
    normw = normw_ref[...]
    r_i = lax.broadcasted_iota(jnp.int32, (chunk, chunk), 0)
    c_i = lax.broadcasted_iota(jnp.int32, (chunk, chunk), 1)
    incl = r_i >= c_i
    strict = r_i > c_i

    heads = range(GDN_HEADS)
    pairs = [(c, h) for c in range(rows // chunk) for h in heads]
    gl_rows = [gc_all[(c + 1) * chunk - 1:(c + 1) * chunk, :] for c in range(rows // chunk)]
    ekg_all = [jnp.exp(gl_rows[c] - gc_all[c * chunk:(c + 1) * chunk, :]) for c in range(rows // chunk)]
    egl_all = [jnp.exp(gl_rows[c]) for c in range(rows // chunk)]
    qgs, kgs, rhss, a_mats, qks, egls = [], [], [], [], [], []
    for c, h in pairs:
        lo = c * chunk
        hi = lo + chunk
        q = act_ref[lo:hi, h * GDN_D:(h + 1) * GDN_D]
        k = act_ref[lo:hi, GDN_HD + h * GDN_D:GDN_HD + (h + 1) * GDN_D]
        v = act_ref[lo:hi, 2 * GDN_HD + h * GDN_D:2 * GDN_HD + (h + 1) * GDN_D]
        q = q * lax.rsqrt(jnp.sum(q * q, axis=-1, keepdims=True) + EPS) * (GDN_D ** -0.5)
        k = k * lax.rsqrt(jnp.sum(k * k, axis=-1, keepdims=True) + EPS)
        beta = beta_all[lo:hi, LANE_BETA + h:LANE_BETA + h + 1]
        gcol = gc_all[lo:hi, LANE_DECAY + h:LANE_DECAY + h + 1]
        egc = egc_all[lo:hi, LANE_DECAY + h:LANE_DECAY + h + 1]
        grow = gc_rows[h:h + 1, lo:hi]
        kb = k * beta
        k16 = k.astype(BF16)
        decay = jnp.where(incl, jnp.exp(jnp.where(incl, gcol - grow, 0.0)), 0.0)
        qgs.append(q * egc)
        kgs.append((k * ekg_all[c][:, LANE_DECAY + h:LANE_DECAY + h + 1]).astype(BF16))
        egls.append(egl_all[c][:, LANE_DECAY + h:LANE_DECAY + h + 1])
        rhss.append(jnp.concatenate([v * beta, kb * egc], axis=1).astype(BF16))
        a_mats.append(jnp.where(strict, _nt_dot(kb.astype(BF16), k16) * decay, 0.0))
        qks.append(jnp.where(incl, _nt_dot(q.astype(BF16), k16) * decay, 0.0).astype(BF16))
    t_invs = _unit_lower_inverses(a_mats, r_i, c_i, chunk)
    uws = [_dot(t_invs[i].astype(BF16), rhss[i]) for i in range(len(pairs))]

    state = [state_ref[h] for h in heads]
    for c in range(rows // chunk):
        idx = [c * GDN_HEADS + h for h in heads]
        wss = [_dot(jnp.concatenate([uws[i][:, GDN_D:], qgs[i]], axis=0).astype(BF16), state[h].astype(BF16))
               for h, i in zip(heads, idx)]
        v_news = [(uws[i][:, :GDN_D] - ws[:chunk]).astype(BF16) for i, ws in zip(idx, wss)]
        outs = [ws[chunk:] + _dot(qks[i], v_new) for i, ws, v_new in zip(idx, wss, v_news)]
        state = [state[h] * egls[i] + _tn_dot(kgs[i], v_new) for h, i, v_new in zip(heads, idx, v_news)]
        for h in heads:
            zg = _silu(z_ref[c * chunk:(c + 1) * chunk, h * GDN_D:(h + 1) * GDN_D])
            o = outs[h]
            o = o * lax.rsqrt(jnp.mean(o * o, axis=-1, keepdims=True) + EPS) * normw * zg
            o_ref[c * chunk:(c + 1) * chunk, h * GDN_D:(h + 1) * GDN_D] = o.astype(o_ref.dtype)
    for h in heads:
        state_ref[h] = state[h]

    @pl.when(j == pl.num_programs(1) - 1)
    def _():
        sout_ref[0] = state_ref[...]


def _gdn(act, z, sm, s0, avec, dtvec, normw, *, batch, seq, rows, chunk):
    nblk = seq // rows
    row_spec = lambda n: pl.BlockSpec((rows, n), lambda b, j: (b * nblk + j, 0))
    const2 = lambda a: pl.BlockSpec(a.shape, lambda b, j: (0, 0))
    body = functools.partial(_gdn_body, rows=rows, chunk=chunk)
    return pl.pallas_call(
        body,
        grid=(batch, nblk),
        in_specs=[row_spec(GDN_QKV), row_spec(GDN_HD), row_spec(LANES),
                  pl.BlockSpec(s0.shape, lambda b, j: (0, 0, 0)), const2(avec), const2(dtvec), const2(normw)],
        out_specs=[row_spec(GDN_HD), pl.BlockSpec((1, GDN_HEADS, GDN_D, GDN_D), lambda b, j: (b, 0, 0, 0))],
        out_shape=[jax.ShapeDtypeStruct((batch * seq, GDN_HD), BF16),
                   jax.ShapeDtypeStruct((batch, GDN_HEADS, GDN_D, GDN_D), F32)],
        scratch_shapes=[pltpu.VMEM((GDN_HEADS, GDN_D, GDN_D), F32)],
        compiler_params=pltpu.CompilerParams(dimension_semantics=("arbitrary", "arbitrary"),
                                             vmem_limit_bytes=VMEM_LIMIT),
        name="gdn",
    )(act, z, sm, s0, avec, dtvec, normw)


def _bias_constants():
    sel = np.zeros((LANES, 2 * FOX_HD), np.float32)
    ones = np.zeros((1, 2 * FOX_HD), np.float32)
    for head in range(FOX_HEADS):
        base = (head // 2) * LANES + BIAS_STRIDE * (head % 2)
        for t in range(3):
            src = LANE_FORGET + t * FOX_HEADS + head
            sel[src, base + t] = 1.0
            sel[src, FOX_HD + base + 3 + t] = -1.0
            ones[0, base + 3 + t] = 1.0
            ones[0, FOX_HD + base + t] = 1.0
    return jnp.asarray(sel, BF16), jnp.asarray(ones)


def _pack3(x):
    hi = x.astype(BF16).astype(F32)
    rest = x - hi
    mid = rest.astype(BF16).astype(F32)
    lo = (rest - mid).astype(BF16).astype(F32)
    return (hi + pltpu.roll(mid, FOX_HEADS, 1) + pltpu.roll(lo, 2 * FOX_HEADS, 1)).astype(BF16)


def _cumgate_body(sm_ref, fb_ref, c0_ref, sel_ref, ones_ref, ccol_ref, qb_ref, kb_ref, *, seq, blk):
    lane = lax.broadcasted_iota(jnp.int32, (1, LANES), 1)
    valid = (lane >= LANE_FORGET) & (lane < LANE_FORGET + FOX_HEADS)
    r_i = lax.broadcasted_iota(jnp.int32, (blk, blk), 0)
    c_i = lax.broadcasted_iota(jnp.int32, (blk, blk), 1)
    tri = jnp.where(r_i >= c_i, 1.0, 0.0).astype(BF16)
    carry = c0_ref[...]
    for i in range(seq // blk):
        xg = sm_ref[i * blk:(i + 1) * blk, :] + fb_ref[...]
        logf = jnp.where(valid, jnp.minimum(xg, 0.0) - jnp.log1p(jnp.exp(-jnp.abs(xg))), 0.0)
        cs3 = _dot(tri, _pack3(logf))
        cs = cs3 + pltpu.roll(cs3, LANES - FOX_HEADS, 1) + pltpu.roll(cs3, LANES - 2 * FOX_HEADS, 1)
        cs = jnp.where(valid, carry + cs, 0.0)
        ccol_ref[i * blk:(i + 1) * blk, :] = cs
        bias = _dot(_pack3(cs * LOG2E), sel_ref[...]) + ones_ref[...]
        qb_ref[i * blk:(i + 1) * blk, :] = bias[:, :FOX_HD].astype(BF16)
        kb_ref[i * blk:(i + 1) * blk, :] = bias[:, FOX_HD:].astype(BF16)
        carry = cs[blk - 1:blk, :]


def _cumgate(sm, fb, c0, consts, *, batch, seq):
    blk = min(CUMGATE_ROWS, seq)
    sel, ones = consts
    body = functools.partial(_cumgate_body, seq=seq, blk=blk)
    c2 = lambda a: pl.BlockSpec(a.shape, lambda b: (0, 0))
    return pl.pallas_call(
        body,
        grid=(batch,),
        in_specs=[pl.BlockSpec((seq, LANES), lambda b: (b, 0)), c2(fb), c2(c0), c2(sel), c2(ones)],
        out_specs=[pl.BlockSpec((seq, LANES), lambda b: (b, 0)), pl.BlockSpec((seq, FOX_HD), lambda b: (b, 0)),
                   pl.BlockSpec((seq, FOX_HD), lambda b: (b, 0))],
        out_shape=[jax.ShapeDtypeStruct((batch * seq, LANES), F32),
                   jax.ShapeDtypeStruct((batch * seq, FOX_HD), BF16),
                   jax.ShapeDtypeStruct((batch * seq, FOX_HD), BF16)],
        compiler_params=pltpu.CompilerParams(dimension_semantics=("arbitrary",), vmem_limit_bytes=VMEM_LIMIT),
        name="cumgate",
    )(sm, fb, c0, sel, ones)


def _fox_body(q_ref, qb_ref, k_ref, kb_ref, v_ref, km_ref, kbm_ref, vm_ref, o_ref,
              s_scr, p_scr, m_scr, alpha_scr, acc_scr, *, tq):
    i = pl.program_id(2)
    wide = tq
    lane = lax.broadcasted_iota(jnp.int32, (1, LANES), 1)
    q_all = q_ref[...]
    qb_all = qb_ref[...]
    zero = jnp.zeros_like(q_all)
    one = jnp.ones_like(q_all[0:1, :])
    own = [lane < FOX_DH, lane >= FOX_DH]
    q_aug = []
    for hh in range(2):
        bias_lanes = (lane >= BIAS_STRIDE * hh) & (lane < BIAS_STRIDE * (hh + 1))
        q_aug.append(jnp.concatenate([jnp.where(own[hh], q_all, zero), jnp.where(bias_lanes, qb_all, zero)], axis=1))

    def keys(start, width):
        return jnp.concatenate([k_ref[pl.ds(start, width), :], kb_ref[pl.ds(start, width), :]], axis=1)

    def values(start, width, hh):
        return jnp.where(own[hh], v_ref[pl.ds(start, width), :], one)

    n_full = i
    two = range(2)

    diag_start = pl.multiple_of(n_full * wide, wide)
    meta_mask = lax.broadcasted_iota(jnp.int32, (tq, LANES), 1) < N_META
    causal = lax.broadcasted_iota(jnp.int32, (tq, wide), 1) <= lax.broadcasted_iota(jnp.int32, (tq, wide), 0)
    km_aug = jnp.concatenate([km_ref[...], kbm_ref[...]], axis=1)
    kd_aug = keys(diag_start, wide)
    s_meta = [jnp.where(meta_mask, _nt_dot(q_aug[hh], km_aug), MASK_VALUE) for hh in two]
    s_diag = [jnp.where(causal, _nt_dot(q_aug[hh], kd_aug), MASK_VALUE) for hh in two]

    n_pairs = (n_full + 1) // 2

    def block_start(t):
        return pl.multiple_of(jnp.clip(t, 0, jnp.maximum(n_full - 1, 0)) * wide, wide)

    def value_start(t):
        return pl.multiple_of(jnp.where(t < 0, diag_start, block_start(t)), wide)

    def pipeline_step(u, carry):
        one_step(2 * u, 0, None)
        one_step(2 * u + 1, 1, 2 * u + 1 < n_full)
        return carry

    def one_step(t, cur, real_block):
        nxt = 1 - cur
        prev_start = value_start(t - 1)
        next_aug = keys(block_start(t + 1), wide)
        for hh in two:
            acc_scr[hh] = alpha_scr[hh] * acc_scr[hh] + _dot(p_scr[nxt, hh], values(prev_start, wide, hh))
        for hh in two:
            for r0 in range(0, tq, SOFTMAX_ROWS):
                rows = slice(r0, r0 + SOFTMAX_ROWS)
                s_tile = s_scr[cur, hh, rows, :]
                m_old = m_scr[hh, rows, :]
                m_new = jnp.maximum(m_old, jnp.max(s_tile, axis=-1, keepdims=True))
                if real_block is not None:
                    m_new = jnp.where(real_block, m_new, m_old)
                alpha_scr[hh, rows, :] = jnp.exp2(m_old - m_new)
                m_scr[hh, rows, :] = m_new
                p_scr[cur, hh, rows, :] = jnp.exp2(s_tile - m_new).astype(BF16)
        for hh in two:
            s_scr[nxt, hh] = _nt_dot(q_aug[hh], next_aug)

    first_aug = keys(block_start(0), wide)
    for hh in two:
        s_scr[0, hh] = _nt_dot(q_aug[hh], first_aug)
    for hh in two:
        m_first = jnp.maximum(jnp.max(s_meta[hh], axis=-1, keepdims=True),
                              jnp.max(s_diag[hh], axis=-1, keepdims=True))
        m_scr[hh] = m_first
        alpha_scr[hh] = jnp.ones((tq, 1), F32)
        p_scr[1, hh] = jnp.exp2(s_diag[hh] - m_first).astype(BF16)
        acc_scr[hh] = _dot(jnp.exp2(s_meta[hh] - m_first).astype(BF16), jnp.where(own[hh], vm_ref[...], one))
    lax.fori_loop(0, n_pairs, pipeline_step, 0)

    t_last = 2 * n_pairs - 1
    last_start = value_start(t_last)
    keep_last = jnp.where(t_last < n_full, 1.0, 0.0)
    outs = []
    for hh in two:
        acc = alpha_scr[hh] * acc_scr[hh] + keep_last * _dot(p_scr[1, hh], values(last_start, wide, hh))
        outs.append(acc / pltpu.roll(acc, FOX_DH, 1))
    o_ref[...] = jnp.where(own[0], outs[0], outs[1]).astype(o_ref.dtype)


def _fox(fq, qb, fk, kb, fv, km, kbm, vm, *, batch, seq, tq):
    nq = seq // tq
    npair = FOX_HEADS // 2
    body = functools.partial(_fox_body, tq=tq)
    qspec = pl.BlockSpec((tq, LANES), lambda b, p, i: (b * nq + i, p))
    kspec = pl.BlockSpec((seq, LANES), lambda b, p, i: (b, p))
    mspec = pl.BlockSpec((LANES, LANES), lambda b, p, i: (0, p))
    return pl.pallas_call(
        body,
        grid=(batch, npair, nq),
        in_specs=[qspec, qspec, kspec, kspec, kspec, mspec, mspec, mspec],
        out_specs=qspec,
        out_shape=jax.ShapeDtypeStruct((batch * seq, FOX_HD), BF16),
        scratch_shapes=[pltpu.VMEM((2, 2, tq, tq), F32), pltpu.VMEM((2, 2, tq, tq), BF16),
                        pltpu.VMEM((2, tq, 1), F32), pltpu.VMEM((2, tq, 1), F32), pltpu.VMEM((2, tq, LANES), F32)],
        compiler_params=pltpu.CompilerParams(dimension_semantics=("arbitrary", "arbitrary", "arbitrary"),
                                             vmem_limit_bytes=VMEM_LIMIT),
        name="fox",
    )(fq, qb, fk, kb, fv, km, kbm, vm)


def _ffn_body(x_ref, og_ref, of_ref, wo_ref, nw_ref, wg_ref, wu_ref, wd_ref, fw_ref, out_ref, act_ref, *, fchunk):
    h1 = x_ref[...] + _dot(og_ref[...], wo_ref[0:GDN_HD, :]) + _dot(of_ref[...], wo_ref[GDN_HD:GDN_HD + FOX_HD, :])
    n = (h1 * lax.rsqrt(jnp.mean(h1 * h1, axis=-1, keepdims=True) + EPS) * nw_ref[...]).astype(BF16)
    d_ff = wg_ref.shape[1]
    for c in range(d_ff // fchunk):
        sl = slice(c * fchunk, (c + 1) * fchunk)
        g = _dot(n, wg_ref[:, sl])
        up = _dot(n, wu_ref[:, sl])
        act_ref[:, sl] = (_silu(g) * up).astype(BF16)
    acc = h1 + _dot(act_ref[...], wd_ref[...])
    out_ref[...] = acc * lax.rsqrt(jnp.mean(acc * acc, axis=-1, keepdims=True) + EPS) * fw_ref[...]


def _ffn(x2d, og, of, wo, nw, wg, wu, wd, fw, tm):
    rows = x2d.shape[0]
    row_spec = lambda n: pl.BlockSpec((tm, n), lambda i: (i, 0))
    const_spec = lambda a: pl.BlockSpec(a.shape, lambda i: (0, 0), pipeline_mode=pl.Buffered(1))
    assert wg.shape[1] % FFN_COLS == 0
    body = functools.partial(_ffn_body, fchunk=FFN_COLS)
    return pl.pallas_call(
        body,
        grid=(rows // tm,),
        in_specs=[row_spec(D_MODEL), row_spec(GDN_HD), row_spec(FOX_HD), const_spec(wo), const_spec(nw),
                  const_spec(wg), const_spec(wu), const_spec(wd), const_spec(fw)],
        out_specs=row_spec(D_MODEL),
        out_shape=jax.ShapeDtypeStruct((rows, D_MODEL), F32),
        scratch_shapes=[pltpu.VMEM((tm, wg.shape[1]), BF16)],
        compiler_params=pltpu.CompilerParams(dimension_semantics=("arbitrary",), vmem_limit_bytes=VMEM_LIMIT),
        name="ffn",
    )(x2d, og, of, wo, nw, wg, wu, wd, fw)


def _lane_vec(vals, offset):
    return jnp.zeros((1, LANES), F32).at[0, offset:offset + vals.shape[0]].set(vals.astype(F32))


def kernel(x, meta_tokens, attn_norm_w, w_in, conv_w, a_log, dt_bias, gdn_norm_w, fgate_b, w_out, ffn_norm_w,
           w_gate, w_up, w_down, final_norm_w):
    batch, seq, _ = x.shape
    assert w_in.shape[0] == 1, "one layer"
    for tile_rows in (INPROJ_ROWS, GDN_ROWS, CUMGATE_ROWS, FOX_ROWS, FFN_ROWS):
        assert seq % tile_rows == 0
    w = w_in[0]
    gate_start = GDN_QKV + GDN_HD
    fox_start = gate_start + 2 * GDN_HEADS
    forget_start = fox_start + 3 * FOX_HD
    assert w.shape == (D_MODEL, forget_start + FOX_HEADS)
    wbig = jnp.concatenate([w[:, :gate_start], w[:, fox_start:forget_start]], axis=1).astype(BF16)
    wsm = jnp.concatenate([w[:, gate_start:fox_start], w[:, forget_start:],
                           jnp.zeros((D_MODEL, LANES - 2 * GDN_HEADS - FOX_HEADS), F32)], axis=1).astype(BF16)
    anw = attn_norm_w[0][None, :]
    avec = _lane_vec(-jnp.exp(a_log[0]), LANE_DECAY)
    dtvec = _lane_vec(dt_bias[0], LANE_DECAY)
    fbvec = _lane_vec(fgate_b[0], LANE_FORGET)
    gnw = gdn_norm_w[0][None, :]
    convw = conv_w[0]
    bias_consts = _bias_constants()

    x2d = x.reshape(batch * seq, D_MODEL)

    zeros_hist = jnp.zeros((SUBLANES, GDN_QKV), F32)
    act_m, z_m, _, fk_m, fv_m, sm_m, tail_m = _inproj(meta_tokens, anw, wbig, wsm, convw, zeros_hist,
                                                      tm=N_META, tiles_per_seq=1)
    zeros_state = jnp.zeros((GDN_HEADS, GDN_D, GDN_D), F32)
    _, s_meta = _gdn(act_m, z_m, sm_m, zeros_state, avec, dtvec, gnw,
                     batch=1, seq=N_META, rows=N_META, chunk=N_META)
    ccol_m, _, kb_m = _cumgate(sm_m, fbvec, jnp.zeros((1, LANES), F32), bias_consts, batch=1, seq=N_META)
    pad_rows = ((0, LANES - N_META), (0, 0))
    km = jnp.pad(fk_m, pad_rows)
    kbm = jnp.pad(kb_m, pad_rows)
    vm = jnp.pad(fv_m, pad_rows)

    act, z, fq, fk, fv, sm, _ = _inproj(x2d, anw, wbig, wsm, convw, tail_m,
                                        tm=INPROJ_ROWS, tiles_per_seq=seq // INPROJ_ROWS)
    o_gdn, _ = _gdn(act, z, sm, s_meta[0], avec, dtvec, gnw, batch=batch, seq=seq, rows=GDN_ROWS, chunk=GDN_CHUNK)
    _, qb, kb = _cumgate(sm, fbvec, ccol_m[N_META - 1:], bias_consts, batch=batch, seq=seq)
    o_fox = _fox(fq, qb, fk, kb, fv, km, kbm, vm, batch=batch, seq=seq, tq=FOX_ROWS)

    out = _ffn(x2d, o_gdn, o_fox, w_out[0].astype(BF16), ffn_norm_w[0][None, :], w_gate[0].astype(BF16),
               w_up[0].astype(BF16), w_down[0].astype(BF16), final_norm_w[None, :], tm=FFN_ROWS)
    return out.reshape(batch, seq, D_MODEL)
```
